```python
import math
import jax
import jax.numpy as jnp
from jax import lax
import numpy as np

D_MODEL = 1024
BATCH = 2
SEQ = 8192
DEPTH = 2
DEC_BATCH = 16
DEC_SEQ = 64
PAST_LEN = 4096

CHUNK = 64
HEAD_DIM = 64
SB_WIDTH = D_MODEL // 2
SB_HEADS = SB_WIDTH // HEAD_DIM
SGU_WIDTH = D_MODEL // 2
SGU_GROUPS = 8
SGU_GROUP_DIM = SGU_WIDTH // SGU_GROUPS
SGU_CHUNK = 128
QBLK = 128
IN_WIDTH = 3 * SB_WIDTH + 2 * SGU_WIDTH
MIX_WIDTH = SB_WIDTH + SGU_WIDTH
SSM_GROUP = 16
SSM_GROUPS = D_MODEL // SSM_GROUP
SSM_STATE = 64
STEP_MIN = 1e-3
STEP_MAX = 1e-1
_FFN_RAW = (8 * D_MODEL + 2) // 3
FFN_HIDDEN = ((_FFN_RAW + 255) // 256) * 256
N_AB = (DEPTH + 1) // 2
N_SSM = DEPTH // 2
RMS_EPS = 1e-6

kernel_name = 'stream_sb_gmlp_s5_step'


def rmsnorm(x, g):
    xf = x.astype(jnp.float32)
    y = xf * lax.rsqrt(jnp.mean(xf * xf, axis=-1, keepdims=True) + RMS_EPS)
    return (y * g.astype(jnp.float32)).astype(x.dtype)


def swiglu(h, w_gate, w_up, w_down):
    return (jax.nn.silu(h @ w_gate) * (h @ w_up)) @ w_down


def ab_project(h, w_in):
    bsz, L = h.shape[:2]
    z = h @ w_in
    q, k, v, u, g = jnp.split(z, [SB_WIDTH, 2 * SB_WIDTH, 3 * SB_WIDTH, 3 * SB_WIDTH + SGU_WIDTH], axis=-1)
    heads = lambda t: t.reshape(bsz, L, SB_HEADS, HEAD_DIM)
    return heads(q), heads(k), heads(v), jax.nn.gelu(u), jax.nn.gelu(g)


def sb_attend(q, k, v, q_pos, k_pos):
    z = jnp.einsum('bqhd,bkhd->bhqk', q.astype(jnp.float32), k.astype(jnp.float32)) * (HEAD_DIM ** -0.5)
    mask = k_pos[None, :] < q_pos[:, None]
    log_stay = jnp.where(mask, jax.nn.log_sigmoid(-z), 0.0)
    log_after = lax.cumsum(log_stay, axis=3, reverse=True) - log_stay
    w = jnp.where(mask, jnp.exp(jax.nn.log_sigmoid(z) + log_after), 0.0)
    return jnp.einsum('bhqk,bkhd->bqhd', w, v.astype(jnp.float32))


def sb_prompt(q, k, v):
    bsz, L = q.shape[:2]
    nblk = L // QBLK
    qb = jnp.moveaxis(q.reshape(bsz, nblk, QBLK, SB_HEADS, HEAD_DIM), 1, 0)
    pos = jnp.arange(L)
    qpos = pos.reshape(nblk, QBLK)
    out = lax.map(lambda a: sb_attend(a[0], k, v, a[1], pos), (qb, qpos))
    return jnp.moveaxis(out, 0, 1).reshape(bsz, L, SB_WIDTH)


def sb_sample(q, k, v, cache_k, cache_v):
    past, n = cache_k.shape[1], q.shape[1]
    k_all = jnp.concatenate([cache_k.astype(k.dtype), k], axis=1)
    v_all = jnp.concatenate([cache_v.astype(v.dtype), v], axis=1)
    out = sb_attend(q, k_all, v_all, past + jnp.arange(n), jnp.arange(past + n))
    return out.reshape(q.shape[0], n, SB_WIDTH)


def sgu_prompt(u, g, w_s, b_s):
    bsz, L = g.shape[:2]
    nc = L // SGU_CHUNK
    tri = jnp.tril(jnp.ones((SGU_CHUNK, SGU_CHUNK), dtype=bool))
    ws = jnp.where(tri, w_s, 0.0).astype(jnp.float32)
    gc = g.reshape(bsz, nc, SGU_CHUNK, SGU_GROUPS, SGU_GROUP_DIM).astype(jnp.float32)
    mixed = jnp.einsum('gts,bcsgd->bctgd', ws, gc) + b_s.T.astype(jnp.float32)[:, :, None]
    return u.astype(jnp.float32) * mixed.reshape(bsz, L, SGU_WIDTH)


def sgu_sample(u, g, w_s, b_s):
    bsz, n = g.shape[:2]
    tri = jnp.tril(jnp.ones((n, n), dtype=bool))
    ws = jnp.where(tri, w_s[:, :n, :n], 0.0).astype(jnp.float32)
    gs = g.reshape(bsz, n, SGU_GROUPS, SGU_GROUP_DIM).astype(jnp.float32)
    mixed = jnp.einsum('gts,bsgd->btgd', ws, gs) + b_s[:, :n].T.astype(jnp.float32)[:, :, None]
    return u.astype(jnp.float32) * mixed.reshape(bsz, n, SGU_WIDTH)


def ab_mixer_prompt(h, w_in, w_s, b_s, w_out):
    q, k, v, u, g = ab_project(h, w_in)
    att = sb_prompt(q, k, v)
    sg = sgu_prompt(u, g, w_s, b_s)
    out = jnp.concatenate([att, sg], axis=-1).astype(h.dtype) @ w_out
    return out, k, v


def ab_mixer_sample(h, cache_k, cache_v, w_in, w_s, b_s, w_out):
    q, k, v, u, g = ab_project(h, w_in)
    att = sb_sample(q, k, v, cache_k, cache_v)
    sg = sgu_sample(u, g, w_s, b_s)
    out = jnp.concatenate([att, sg], axis=-1).astype(h.dtype) @ w_out
    return out, k, v, g


def _cplx_affine_combine(e1, e2):
    a1r, a1i, b1r, b1i = e1
    a2r, a2i, b2r, b2i = e2
    return (a1r * a2r - a1i * a2i,
            a1r * a2i + a1i * a2r,
            a2r * b1r - a2i * b1i + b2r,
            a2r * b1i + a2i * b1r + b2i)


def s5_mixer(h, h0_re, h0_im, lam_re, lam_im, log_step, b_re, b_im, c_re, c_im, d_skip, w_glu):
    f32 = jnp.float32
    bsz, L = h.shape[:2]
    u = h.astype(f32)
    lr, li = lam_re.astype(f32), lam_im.astype(f32)
    step = jnp.exp(log_step.astype(f32))[:, None]
    mag, ang = jnp.exp(lr * step), li * step
    ab_re, ab_im = mag * jnp.cos(ang), mag * jnp.sin(ang)
    den = lr * lr + li * li
    nr = ab_re - 1.0
    co_re = (nr * lr + ab_im * li) / den
    co_im = (ab_im * lr - nr * li) / den
    br, bi = b_re.astype(f32), b_im.astype(f32)
    bb_re = co_re[..., None] * br - co_im[..., None] * bi
    bb_im = co_re[..., None] * bi + co_im[..., None] * br
    ug = u.reshape(bsz, L, SSM_GROUPS, SSM_GROUP)
    x_re = jnp.einsum('gpc,blgc->blgp', bb_re, ug)
    x_im = jnp.einsum('gpc,blgc->blgp', bb_im, ug)
    h0r, h0i = h0_re.astype(f32), h0_im.astype(f32)
    x_re = x_re.at[:, 0].add(ab_re * h0r - ab_im * h0i)
    x_im = x_im.at[:, 0].add(ab_re * h0i + ab_im * h0r)
    a_re = jnp.broadcast_to(ab_re, x_re.shape)
    a_im = jnp.broadcast_to(ab_im, x_im.shape)
    _, _, s_re, s_im = lax.associative_scan(_cplx_affine_combine, (a_re, a_im, x_re, x_im), axis=1)
    y = (jnp.einsum('gcp,blgp->blgc', c_re.astype(f32), s_re)
         - jnp.einsum('gcp,blgp->blgc', c_im.astype(f32), s_im))
    y = y.reshape(bsz, L, D_MODEL) + d_skip.astype(f32) * u
    gl = jax.nn.gelu(y).astype(h.dtype) @ w_glu
    ga, gb = jnp.split(gl, 2, axis=-1)
    out = (ga.astype(f32) * jax.nn.sigmoid(gb.astype(f32))).astype(h.dtype)
    return out, s_re[:, -1], s_im[:, -1]


def setup_inputs(seed: int = 0) -> dict:
    key = jax.random.key(seed)
    ks = iter(jax.random.split(key, 32))
    nrm = lambda shape, scale: jax.random.normal(next(ks), shape, jnp.float32) * scale
    x_prompt = nrm((BATCH, SEQ, D_MODEL), 1.0)
    x_sample = nrm((DEC_BATCH, DEC_SEQ, D_MODEL), 1.0)
    cache_sb_k = nrm((N_AB, DEC_BATCH, PAST_LEN, SB_HEADS, HEAD_DIM), 1.0)
    cache_sb_v = nrm((N_AB, DEC_BATCH, PAST_LEN, SB_HEADS, HEAD_DIM), 1.0)
    state_ssm_re = nrm((N_SSM, DEC_BATCH, SSM_GROUPS, SSM_STATE), 0.1)
    state_ssm_im = nrm((N_SSM, DEC_BATCH, SSM_GROUPS, SSM_STATE), 0.1)
    norm_mix = 1.0 + nrm((DEPTH, D_MODEL), 0.02)
    norm_ffn = 1.0 + nrm((DEPTH, D_MODEL), 0.02)
    norm_final = 1.0 + nrm((D_MODEL,), 0.02)
    ab_w_in = nrm((N_AB, D_MODEL, IN_WIDTH), D_MODEL ** -0.5)
    sgu_w = nrm((N_AB, SGU_GROUPS, SGU_CHUNK, SGU_CHUNK), SGU_CHUNK ** -0.5)
    sgu_b = 1.0 + nrm((N_AB, SGU_GROUPS, SGU_CHUNK), 0.02)
    ab_w_out = nrm((N_AB, MIX_WIDTH, D_MODEL), MIX_WIDTH ** -0.5)
    ssm_lam_re = -0.5 + nrm((N_SSM, SSM_GROUPS, SSM_STATE), 0.01)
    ssm_lam_im = jnp.pi * jnp.arange(SSM_STATE, dtype=jnp.float32) + nrm((N_SSM, SSM_GROUPS, SSM_STATE), 0.01)
    ssm_log_step = jax.random.uniform(next(ks), (N_SSM, SSM_GROUPS), jnp.float32,
                                      math.log(STEP_MIN), math.log(STEP_MAX))
    ssm_b_re = nrm((N_SSM, SSM_GROUPS, SSM_STATE, SSM_GROUP), (2 * SSM_GROUP) ** -0.5)
    ssm_b_im = nrm((N_SSM, SSM_GROUPS, SSM_STATE, SSM_GROUP), (2 * SSM_GROUP) ** -0.5)
    ssm_c_re = nrm((N_SSM, SSM_GROUPS, SSM_GROUP, SSM_STATE), SSM_STATE ** -0.5)
    ssm_c_im = nrm((N_SSM, SSM_GROUPS, SSM_GROUP, SSM_STATE), SSM_STATE ** -0.5)
    ssm_d = nrm((N_SSM, D_MODEL), 1.0)
    ssm_w_glu = nrm((N_SSM, D_MODEL, 2 * D_MODEL), D_MODEL ** -0.5)
    ffn_w_gate = nrm((DEPTH, D_MODEL, FFN_HIDDEN), D_MODEL ** -0.5)
    ffn_w_up = nrm((DEPTH, D_MODEL, FFN_HIDDEN), D_MODEL ** -0.5)
    ffn_w_down = nrm((DEPTH, FFN_HIDDEN, D_MODEL), FFN_HIDDEN ** -0.5)
    return {'x_prompt': x_prompt, 'x_sample': x_sample,
            'cache_sb_k': cache_sb_k, 'cache_sb_v': cache_sb_v,
            'state_ssm_re': state_ssm_re, 'state_ssm_im': state_ssm_im,
            'norm_mix': norm_mix, 'norm_ffn': norm_ffn, 'norm_final': norm_final,
            'ab_w_in': ab_w_in, 'sgu_w': sgu_w, 'sgu_b': sgu_b, 'ab_w_out': ab_w_out,
            'ssm_lam_re': ssm_lam_re, 'ssm_lam_im': ssm_lam_im, 'ssm_log_step': ssm_log_step,
            'ssm_b_re': ssm_b_re, 'ssm_b_im': ssm_b_im, 'ssm_c_re': ssm_c_re, 'ssm_c_im': ssm_c_im,
            'ssm_d': ssm_d, 'ssm_w_glu': ssm_w_glu,
            'ffn_w_gate': ffn_w_gate, 'ffn_w_up': ffn_w_up, 'ffn_w_down': ffn_w_down}


def reference(x_prompt, x_sample, cache_sb_k, cache_sb_v, state_ssm_re, state_ssm_im,
              norm_mix, norm_ffn, norm_final, ab_w_in, sgu_w, sgu_b, ab_w_out,
              ssm_lam_re, ssm_lam_im, ssm_log_step, ssm_b_re, ssm_b_im, ssm_c_re, ssm_c_im,
              ssm_d, ssm_w_glu, ffn_w_gate, ffn_w_up, ffn_w_down):
    xp, xs = x_prompt, x_sample
    k_p, v_p, k_s, v_s, g_s = [], [], [], [], []
    sr_p, si_p, sr_s, si_s = [], [], [], []
    for layer in range(DEPTH):
        i = layer // 2
        hp = rmsnorm(xp, norm_mix[layer])
        hs = rmsnorm(xs, norm_mix[layer])
        if layer % 2 == 0:
            mp, kp, vp = ab_mixer_prompt(hp, ab_w_in[i], sgu_w[i], sgu_b[i], ab_w_out[i])
            ms, kn, vn, gn = ab_mixer_sample(hs, cache_sb_k[i], cache_sb_v[i],
                                             ab_w_in[i], sgu_w[i], sgu_b[i], ab_w_out[i])
            k_p.append(kp); v_p.append(vp); k_s.append(kn); v_s.append(vn); g_s.append(gn)
        else:
            zeros = jnp.zeros((xp.shape[0], SSM_GROUPS, SSM_STATE), jnp.float32)
            mp, rp, ip = s5_mixer(hp, zeros, zeros, ssm_lam_re[i], ssm_lam_im[i], ssm_log_step[i],
                                  ssm_b_re[i], ssm_b_im[i], ssm_c_re[i], ssm_c_im[i], ssm_d[i], ssm_w_glu[i])
            ms, rs, is_ = s5_mixer(hs, state_ssm_re[i], state_ssm_im[i], ssm_lam_re[i], ssm_lam_im[i],
                                   ssm_log_step[i], ssm_b_re[i], ssm_b_im[i], ssm_c_re[i], ssm_c_im[i],
                                   ssm_d[i], ssm_w_glu[i])
            sr_p.append(rp); si_p.append(ip); sr_s.append(rs); si_s.append(is_)
        xp = xp + mp
        xs = xs + ms
        xp = xp + swiglu(rmsnorm(xp, norm_ffn[layer]), ffn_w_gate[layer], ffn_w_up[layer], ffn_w_down[layer])
        xs = xs + swiglu(rmsnorm(xs, norm_ffn[layer]), ffn_w_gate[layer], ffn_w_up[layer], ffn_w_down[layer])
    y_prompt = rmsnorm(xp, norm_final)
    y_sample = rmsnorm(xs, norm_final)
    return (y_prompt, y_sample,
            jnp.stack(k_p), jnp.stack(v_p), jnp.stack(k_s), jnp.stack(v_s), jnp.stack(g_s),
            jnp.stack(sr_p), jnp.stack(si_p), jnp.stack(sr_s), jnp.stack(si_s))
```

```python
import functools
import math

import jax
import jax.numpy as jnp
from jax import lax
from jax.experimental import pallas as pl
from jax.experimental.pallas import tpu as pltpu

F32 = jnp.float32
BF16 = jnp.bfloat16

LANES = 128
VMEM_LIMIT = 56 * 1024 * 1024

D_MODEL = 1024
HEAD_DIM = 64
SB_WIDTH = 512
SGU_WIDTH = 512
SGU_GROUPS = 8
SGU_CHUNK = 128
SSM_GROUP = 16
SSM_STATE = 64
FFN_HIDDEN = 2816
RMS_EPS = 1e-6

TOKEN_TILE = 512
SB_BLOCK = 128
SB_CUTOFF = -104.0
S5_T = 16
S5_TILE_STATES = (LANES // SSM_GROUP) * SSM_STATE


def _rmsnorm(x, g):
    return x * lax.rsqrt(jnp.mean(x * x, axis=-1, keepdims=True) + RMS_EPS) * g


def _gelu(x):
    return 0.5 * x * (1.0 + jnp.tanh(math.sqrt(2.0 / math.pi) * (x + 0.044715 * (x * x * x))))


def _sigmoid(x):
    return 1.0 / (1.0 + jnp.exp(-x))


def _dot(a, b):
    return jnp.dot(a, b, preferred_element_type=F32)


def _params(*sem):
    return pltpu.CompilerParams(dimension_semantics=sem, vmem_limit_bytes=VMEM_LIMIT)


def _resident(shape):
    nd = len(shape)
    return pl.BlockSpec(shape, lambda *_: (0,) * nd, pipeline_mode=pl.Buffered(1))


def _proj_kernel(x_ref, g_ref, w_ref, q_ref, k_ref, v_ref, u_ref, gv_ref):
    h = _rmsnorm(x_ref[...], g_ref[...]).astype(BF16)
    for i, o_ref in enumerate((q_ref, k_ref, v_ref, u_ref, gv_ref)):
        z = _dot(h, w_ref[:, i * SB_WIDTH:(i + 1) * SB_WIDTH])
        if i >= 3:
            z = _gelu(z)
        o_ref[...] = z.astype(o_ref.dtype)


def _proj(x, g, w_bf16):
    m = x.shape[0]
    tile = lambda w: pl.BlockSpec((TOKEN_TILE, w), lambda i: (i, 0))
    return pl.pallas_call(
        _proj_kernel,
        grid=(m // TOKEN_TILE,),
        in_specs=[tile(D_MODEL), _resident((1, D_MODEL)), _resident(w_bf16.shape)],
        out_specs=[tile(SB_WIDTH)] * 5,
        out_shape=[jax.ShapeDtypeStruct((m, SB_WIDTH), dt) for dt in (BF16, F32, F32, F32, F32)],
        compiler_params=_params("parallel"),
        name="proj",
    )(x, g, w_bf16)


def _strict_upper(n):
    j = lax.broadcasted_iota(jnp.int32, (n, n), 0)
    s = lax.broadcasted_iota(jnp.int32, (n, n), 1)
    return jnp.where(j > s, 1.0, 0.0).astype(BF16)


def _sb_tile(qh, kb, vb, c, suffix_mat, mask):
    z = lax.dot_general(qh, kb, (((1,), (1,)), ((), ())), preferred_element_type=F32) * (HEAD_DIM ** -0.5)
    log_beta = jnp.minimum(z, 0.0) - jnp.log1p(jnp.exp(-jnp.abs(z)))
    log_stay = log_beta - z
    if mask is not None:
        log_stay = jnp.where(mask, log_stay, 0.0)
    hi = log_stay.astype(BF16)
    lo = (log_stay - hi.astype(F32)).astype(BF16)
    after = _dot(hi, suffix_mat) + _dot(lo, suffix_mat)
    w = jnp.exp(log_beta + after + c)
    if mask is not None:
        w = jnp.where(mask, w, 0.0)
    return _dot(w.astype(BF16), vb), c + jnp.sum(log_stay, axis=-1, keepdims=True)


def _split_heads(q):
    lane = lax.broadcasted_iota(jnp.int32, q.shape, 1)
    zero = jnp.zeros_like(q)
    return jnp.where(lane < HEAD_DIM, q, zero), jnp.where(lane < HEAD_DIM, zero, q)


def _sb_sweep(q0, q1, carry, k_ref, v_ref, first_block):
    suffix_mat = _strict_upper(SB_BLOCK)

    def cond(state):
        kb, c0, c1, _, _ = state
        return jnp.logical_and(kb >= 0, jnp.maximum(jnp.max(c0), jnp.max(c1)) > SB_CUTOFF)

    def body(state):
        kb, c0, c1, a0, a1 = state
        rows = pl.ds(pl.multiple_of(kb * SB_BLOCK, SB_BLOCK), SB_BLOCK)
        kblk = k_ref[rows, :].astype(BF16)
        vblk = v_ref[rows, :].astype(BF16)
        p0, c0 = _sb_tile(q0, kblk, vblk, c0, suffix_mat, None)
        p1, c1 = _sb_tile(q1, kblk, vblk, c1, suffix_mat, None)
        return kb - 1, c0, c1, a0 + p0, a1 + p1

    return lax.while_loop(cond, body, (first_block,) + tuple(carry))


def _merge_heads(a0, a1):
    lane = lax.broadcasted_iota(jnp.int32, a0.shape, 1)
    return jnp.where(lane < HEAD_DIM, a0, a1)


def _sb_prompt_kernel(q_ref, k_ref, v_ref, o_ref):
    qb = pl.program_id(2)
    q0, q1 = _split_heads(q_ref[...])
    rows = pl.ds(pl.multiple_of(qb * SB_BLOCK, SB_BLOCK), SB_BLOCK)
    kblk = k_ref[rows, :].astype(BF16)
    vblk = v_ref[rows, :].astype(BF16)
    t = lax.broadcasted_iota(jnp.int32, (SB_BLOCK, SB_BLOCK), 0)
    s = lax.broadcasted_iota(jnp.int32, (SB_BLOCK, SB_BLOCK), 1)
    mask = s < t
    suffix_mat = _strict_upper(SB_BLOCK)
    c_init = jnp.zeros((SB_BLOCK, 1), F32)
    a0, c0 = _sb_tile(q0, kblk, vblk, c_init, suffix_mat, mask)
    a1, c1 = _sb_tile(q1, kblk, vblk, c_init, suffix_mat, mask)
    _, _, _, a0, a1 = _sb_sweep(q0, q1, (c0, c1, a0, a1), k_ref, v_ref, qb - 1)
    o_ref[...] = _merge_heads(a0, a1)


def _sb_prompt(q, k, v, bsz, seq):
    nq = seq // SB_BLOCK
    qspec = pl.BlockSpec((SB_BLOCK, LANES), lambda b, hp, i: (b * nq + i, hp))
    kvspec = pl.BlockSpec((seq, LANES), lambda b, hp, i: (b, hp))
    return pl.pallas_call(
        _sb_prompt_kernel,
        grid=(bsz, SB_WIDTH // LANES, nq),
        in_specs=[qspec, kvspec, kvspec],
        out_specs=qspec,
        out_shape=jax.ShapeDtypeStruct((bsz * seq, SB_WIDTH), F32),
        compiler_params=_params("parallel", "parallel", "arbitrary"),
        name="sb_prompt",
    )(q, k, v)


def _sb_sample_kernel(q_ref, k_ref, v_ref, ck_ref, cv_ref, o_ref):
    n = q_ref.shape[0]
    q0, q1 = _split_heads(q_ref[...])
    kblk = k_ref[...].astype(BF16)
    vblk = v_ref[...].astype(BF16)
    t = lax.broadcasted_iota(jnp.int32, (n, n), 0)
    s = lax.broadcasted_iota(jnp.int32, (n, n), 1)
    mask = s < t
    suffix_mat = _strict_upper(n)
    c_init = jnp.zeros((n, 1), F32)
    a0, c0 = _sb_tile(q0, kblk, vblk, c_init, suffix_mat, mask)
    a1, c1 = _sb_tile(q1, kblk, vblk, c_init, suffix_mat, mask)
    last = ck_ref.shape[0] // SB_BLOCK - 1
    _, _, _, a0, a1 = _sb_sweep(q0, q1, (c0, c1, a0, a1), ck_ref, cv_ref, jnp.int32(last))
    o_ref[...] = _merge_heads(a0, a1)


def _sb_sample(q, k, v, cache_k, cache_v, bsz, n):
    past = cache_k.shape[1]
    new = pl.BlockSpec((n, LANES), lambda b, hp: (b, hp))
    old = pl.BlockSpec((None, past, LANES), lambda b, hp: (b, 0, hp))
    return pl.pallas_call(
        _sb_sample_kernel,
        grid=(bsz, SB_WIDTH // LANES),
        in_specs=[new, new, new, old, old],
        out_specs=new,
        out_shape=jax.ShapeDtypeStruct((bsz * n, SB_WIDTH), F32),
        compiler_params=_params("parallel", "parallel"),
        name="sb_sample",
    )(q, k, v, cache_k, cache_v)


def _mix_out_kernel(x_ref, att_ref, u_ref, gv_ref, ws_ref, bs_ref, wo_ref, o_ref, sg_ref):
    chunk = ws_ref.shape[1]
    t = lax.broadcasted_iota(jnp.int32, (chunk, chunk), 0)
    s = lax.broadcasted_iota(jnp.int32, (chunk, chunk), 1)
    tri = s <= t
    ws = [jnp.where(tri, ws_ref[g], 0.0).astype(BF16) for g in range(SGU_GROUPS)]
    lane = lax.broadcasted_iota(jnp.int32, (chunk, LANES), 1)
    group_dim = SGU_WIDTH // SGU_GROUPS
    for c in range(TOKEN_TILE // chunk):
        rows = slice(c * chunk, (c + 1) * chunk)
        for p in range(SGU_WIDTH // LANES):
            cols = slice(p * LANES, (p + 1) * LANES)
            gv = gv_ref[rows, cols].astype(BF16)
            mixed = jnp.where(lane < group_dim, _dot(ws[2 * p], gv), _dot(ws[2 * p + 1], gv)) + bs_ref[:, cols]
            sg_ref[rows, cols] = (u_ref[rows, cols] * mixed).astype(BF16)
    o_ref[...] = (x_ref[...] + _dot(att_ref[...].astype(BF16), wo_ref[:SB_WIDTH, :])
                  + _dot(sg_ref[...], wo_ref[SB_WIDTH:, :]))


def _mix_out(x, att, u, gv, ws, bs_rows, wo_bf16):
    m = x.shape[0]
    tile = lambda w: pl.BlockSpec((TOKEN_TILE, w), lambda i: (i, 0))
    return pl.pallas_call(
        _mix_out_kernel,
        grid=(m // TOKEN_TILE,),
        in_specs=[tile(D_MODEL), tile(SB_WIDTH), tile(SGU_WIDTH), tile(SGU_WIDTH),
                  _resident(ws.shape), _resident(bs_rows.shape), _resident(wo_bf16.shape)],
        out_specs=tile(D_MODEL),
        out_shape=jax.ShapeDtypeStruct((m, D_MODEL), F32),
        scratch_shapes=[pltpu.VMEM((TOKEN_TILE, SGU_WIDTH), BF16)],
        compiler_params=_params("parallel"),
        name="mix_out",
    )(x, att, u, gv, ws, bs_rows, wo_bf16)


FFN_CHUNK = FFN_HIDDEN // 2


def _ffn_kernel(x_ref, g_ref, wg_ref, wu_ref, wd_ref, gn_ref, *o_refs):
    x = x_ref[...]
    h = _rmsnorm(x, g_ref[...]).astype(BF16)
    acc = x
    for c in range(FFN_HIDDEN // FFN_CHUNK):
        cols = slice(c * FFN_CHUNK, (c + 1) * FFN_CHUNK)
        gate = _dot(h, wg_ref[:, cols])
        up = _dot(h, wu_ref[:, cols])
        act = (gate * _sigmoid(gate) * up).astype(BF16)
        acc = acc + _dot(act, wd_ref[cols, :])
    if len(o_refs) == 2:
        o_refs[0][...] = acc
    o_refs[-1][...] = _rmsnorm(acc, gn_ref[...])


def _ffn(x, g, wg, wu, wd, g_next, emit_x):
    m = x.shape[0]
    tile = pl.BlockSpec((TOKEN_TILE, D_MODEL), lambda i: (i, 0))
    n_out = 2 if emit_x else 1
    return pl.pallas_call(
        _ffn_kernel,
        grid=(m // TOKEN_TILE,),
        in_specs=[tile, _resident((1, D_MODEL)), _resident(wg.shape), _resident(wu.shape),
                  _resident(wd.shape), _resident((1, D_MODEL))],
        out_specs=[tile] * n_out,
        out_shape=[jax.ShapeDtypeStruct((m, D_MODEL), F32)] * n_out,
        compiler_params=_params("parallel"),
        name="ffn",
    )(x, g, wg, wu, wd, g_next)


def _s5_kernel(h_ref, re0_ref, im0_ref, klag_ref, f_ref, e_ref, at_ref, d_ref,
               y_ref, reo_ref, imo_ref, wint_ref, xend_ref, sprev_ref, s_ref, *, nseq, seq_rows):
    first_of_tile = jnp.logical_and(pl.program_id(1) == 0, pl.program_id(2) == 0)
    nchunk = seq_rows // S5_T
    half = S5_TILE_STATES

    @pl.when(first_of_tile)
    def _():
        wint_ref[...] = jnp.zeros_like(wint_ref)
        for j in range(S5_T):
            for t in range(j, S5_T):
                wint_ref[j * LANES:(j + 1) * LANES, t * LANES:(t + 1) * LANES] = klag_ref[t - j].astype(BF16)

    @pl.when(pl.program_id(2) == 0)
    def _():
        s_ref[:, :half] = re0_ref[...]
        s_ref[:, half:] = im0_ref[...]

    def slab(ref, t):
        if nseq == 1:
            return [pl.ds(t, nchunk, stride=S5_T)]
        return [pl.ds(n * S5_T + t, nseq, stride=seq_rows) for n in range(nchunk)]

    u = []
    for t in range(S5_T):
        parts = [h_ref[rows, :] for rows in slab(h_ref, t)]
        u.append(parts[0] if len(parts) == 1 else jnp.concatenate(parts, axis=0))
    uc = jnp.concatenate([ut.astype(BF16) for ut in u], axis=1)

    xend_ref[...] = _dot(uc, f_ref[...])

    a1 = at_ref[0:1, :]
    a2 = at_ref[1:2, :]

    def step(n, s):
        rows = pl.ds(n * nseq, nseq)
        sprev_ref[rows, :] = s
        swapped = jnp.concatenate([s[:, half:], s[:, :half]], axis=1)
        return a1 * s + a2 * swapped + xend_ref[rows, :]

    s_fin = lax.fori_loop(0, nchunk, step, s_ref[...])
    s_ref[...] = s_fin

    @pl.when(pl.program_id(2) == pl.num_programs(2) - 1)
    def _():
        reo_ref[...] = s_fin[:, :half]
        imo_ref[...] = s_fin[:, half:]

    sprev = sprev_ref[...].astype(BF16)
    d = d_ref[...]
    pair = 2 * LANES
    for cp in range(S5_T // 2):
        depth = (2 * cp + 2) * LANES
        cols = slice(cp * pair, (cp + 1) * pair)
        y2 = _dot(uc[:, :depth], wint_ref[:depth, cols]) + _dot(sprev, e_ref[:, cols])
        for i in range(2):
            t = 2 * cp + i
            yt = _gelu(y2[:, i * LANES:(i + 1) * LANES] + d * u[t])
            off = 0
            for rows in slab(y_ref, t):
                y_ref[rows, :] = yt[off:off + rows.size, :]
                off += rows.size


def _s5(h, re0, im0, klag, f_tab, e_tab, a_t, d_skip, nseq, seq_rows, blocks_per_seq):
    tokens = h.shape[0]
    nb = re0.shape[0]
    ntile = D_MODEL // LANES
    half = S5_TILE_STATES
    rows = nseq * seq_rows
    nrow = rows // S5_T
    hspec = pl.BlockSpec((rows, LANES), lambda k, b, j: (b * blocks_per_seq + j, k))
    sspec = pl.BlockSpec((None, nseq, half), lambda k, b, j: (b, 0, k))
    per_tile = lambda shape: pl.BlockSpec((None,) + shape, lambda k, b, j: (k,) + (0,) * len(shape))
    kern = functools.partial(_s5_kernel, nseq=nseq, seq_rows=seq_rows)
    return pl.pallas_call(
        kern,
        grid=(ntile, nb, blocks_per_seq),
        in_specs=[hspec, sspec, sspec,
                  per_tile((S5_T, LANES, LANES)), per_tile((S5_T * LANES, 2 * half)),
                  per_tile((2 * half, S5_T * LANES)), per_tile((2, 2 * half)),
                  pl.BlockSpec((1, LANES), lambda k, b, j: (0, k))],
        out_specs=[hspec, sspec, sspec],
        out_shape=[jax.ShapeDtypeStruct((tokens, D_MODEL), F32),
                   jax.ShapeDtypeStruct(re0.shape, F32), jax.ShapeDtypeStruct(im0.shape, F32)],
        scratch_shapes=[pltpu.VMEM((S5_T * LANES, S5_T * LANES), BF16),
                        pltpu.VMEM((nrow, 2 * half), F32),
                        pltpu.VMEM((nrow, 2 * half), F32),
                        pltpu.VMEM((nseq, 2 * half), F32)],
        compiler_params=_params("arbitrary", "arbitrary", "arbitrary"),
        name="s5",
    )(h, re0, im0, klag, f_tab, e_tab, a_t, d_skip)


def _s5_tables(lam_re, lam_im, log_step, b_re, b_im, c_re, c_im):
    groups, nstate = lam_re.shape
    gpt = LANES // SSM_GROUP
    ntile = groups // gpt
    step = jnp.exp(log_step)[:, None]
    mag, ang = jnp.exp(lam_re * step), lam_im * step
    ar, ai = mag * jnp.cos(ang), mag * jnp.sin(ang)
    den = lam_re * lam_re + lam_im * lam_im
    nr = ar - 1.0
    co_re = (nr * lam_re + ai * lam_im) / den
    co_im = (ai * lam_re - nr * lam_im) / den
    bb_re = co_re[..., None] * b_re - co_im[..., None] * b_im
    bb_im = co_re[..., None] * b_im + co_im[..., None] * b_re
    pr, pi = [jnp.ones_like(ar)], [jnp.zeros_like(ar)]
    for _ in range(S5_T):
        pr, pi = pr + [pr[-1] * ar - pi[-1] * ai], pi + [pr[-1] * ai + pi[-1] * ar]
    pw_re, pw_im = jnp.stack(pr), jnp.stack(pi)
    eye = jnp.eye(gpt, dtype=F32)

    def lanes_by_states(val):
        v = val.reshape(val.shape[0], ntile, gpt, nstate, SSM_GROUP)
        return jnp.einsum('tkgpc,gh->ktgchp', v, eye).reshape(ntile, val.shape[0], LANES, gpt * nstate)

    def states_by_lanes(val):
        v = val.reshape(val.shape[0], ntile, gpt, SSM_GROUP, nstate)
        return jnp.einsum('tkgcp,gh->ktgphc', v, eye).reshape(ntile, val.shape[0], gpt * nstate, LANES)

    g_re = pw_re[:S5_T, :, :, None] * bb_re - pw_im[:S5_T, :, :, None] * bb_im
    g_im = pw_re[:S5_T, :, :, None] * bb_im + pw_im[:S5_T, :, :, None] * bb_re
    gt = jnp.concatenate([lanes_by_states(g_re), lanes_by_states(g_im)], axis=-1)
    f_tab = gt[:, ::-1].reshape(ntile, S5_T * LANES, 2 * gpt * nstate)

    qr, qi = pw_re[1:, :, None, :], pw_im[1:, :, None, :]
    e_top = states_by_lanes(c_re * qr - c_im * qi)
    e_bot = states_by_lanes(-c_re * qi - c_im * qr)
    e_tab = jnp.concatenate([e_top, e_bot], axis=2)
    e_tab = jnp.moveaxis(e_tab, 1, 2).reshape(ntile, 2 * gpt * nstate, S5_T * LANES)

    klag = (jnp.einsum('tgpc,gdp->tgcd', g_re, c_re, precision=lax.Precision.HIGHEST)
            - jnp.einsum('tgpc,gdp->tgcd', g_im, c_im, precision=lax.Precision.HIGHEST))
    kv = klag.reshape(S5_T, ntile, gpt, SSM_GROUP, SSM_GROUP)
    klag = jnp.einsum('tkgcd,gh->ktgchd', kv, eye).reshape(ntile, S5_T, LANES, LANES)

    at_re = pw_re[S5_T].reshape(ntile, 1, gpt * nstate)
    at_im = pw_im[S5_T].reshape(ntile, 1, gpt * nstate)
    a_t = jnp.concatenate([jnp.concatenate([at_re, at_re], -1), jnp.concatenate([-at_im, at_im], -1)], axis=1)
    return klag, f_tab.astype(BF16), e_tab.astype(BF16), a_t


def _glu_kernel(x_ref, y_ref, w_ref, o_ref):
    y = y_ref[...].astype(BF16)
    ga = _dot(y, w_ref[:, :D_MODEL])
    gb = _dot(y, w_ref[:, D_MODEL:])
    o_ref[...] = x_ref[...] + ga * _sigmoid(gb)


def _glu(x, y, w_bf16):
    m = x.shape[0]
    tile = pl.BlockSpec((TOKEN_TILE, D_MODEL), lambda i: (i, 0))
    return pl.pallas_call(
        _glu_kernel,
        grid=(m // TOKEN_TILE,),
        in_specs=[tile, tile, _resident(w_bf16.shape)],
        out_specs=tile,
        out_shape=jax.ShapeDtypeStruct((m, D_MODEL), F32),
        compiler_params=_params("parallel"),
        name="glu",
    )(x, y, w_bf16)


def kernel(x_prompt, x_sample, cache_sb_k, cache_sb_v, state_ssm_re, state_ssm_im, norm_mix, norm_ffn,
           norm_final, ab_w_in, sgu_w, sgu_b, ab_w_out, ssm_lam_re, ssm_lam_im, ssm_log_step, ssm_b_re,
           ssm_b_im, ssm_c_re, ssm_c_im, ssm_d, ssm_w_glu, ffn_w_gate, ffn_w_up, ffn_w_down):
    bsz, seq, _ = x_prompt.shape
    dbsz, dseq, _ = x_sample.shape
    past = cache_sb_k.shape[2]
    heads = SB_WIDTH // HEAD_DIM
    row = lambda v: v.reshape(1, -1)

    xp = x_prompt.reshape(bsz * seq, D_MODEL)
    xs = x_sample.reshape(dbsz * dseq, D_MODEL)

    w_in = ab_w_in[0].astype(BF16)
    w_out = ab_w_out[0].astype(BF16)
    bs_rows = jnp.repeat(sgu_b[0].T, SGU_WIDTH // SGU_GROUPS, axis=1)
    g_mix0 = row(norm_mix[0])

    qp, kp, vp, up, gp = _proj(xp, g_mix0, w_in)
    qs, ks, vs, us, gs = _proj(xs, g_mix0, w_in)
    att_p = _sb_prompt(qp, kp, vp, bsz, seq)
    att_s = _sb_sample(qs, ks, vs, cache_sb_k[0].reshape(dbsz, past, SB_WIDTH),
                       cache_sb_v[0].reshape(dbsz, past, SB_WIDTH), dbsz, dseq)
    xp = _mix_out(xp, att_p, up, gp, sgu_w[0], bs_rows, w_out)
    xs = _mix_out(xs, att_s, us, gs, sgu_w[0][:, :dseq, :dseq], bs_rows[:dseq], w_out)

    ffn_w = [(ffn_w_gate[l].astype(BF16), ffn_w_up[l].astype(BF16), ffn_w_down[l].astype(BF16)) for l in range(2)]
    g_mix1 = row(norm_mix[1])
    xp, hp = _ffn(xp, row(norm_ffn[0]), *ffn_w[0], g_mix1, True)
    xs, hs = _ffn(xs, row(norm_ffn[0]), *ffn_w[0], g_mix1, True)

    klag, f_tab, e_tab, a_t = _s5_tables(ssm_lam_re[0], ssm_lam_im[0], ssm_log_step[0], ssm_b_re[0],
                                         ssm_b_im[0], ssm_c_re[0], ssm_c_im[0])
    d_skip = row(ssm_d[0])
    nstates = state_ssm_re.shape[2] * state_ssm_re.shape[3]
    zeros = jnp.zeros((bsz, 1, nstates), F32)
    prompt_block = 4096
    yp, rp, ip = _s5(hp, zeros, zeros, klag, f_tab, e_tab, a_t, d_skip, 1, prompt_block, seq // prompt_block)
    ys, rs, is_ = _s5(hs, state_ssm_re[0].reshape(1, dbsz, nstates), state_ssm_im[0].reshape(1, dbsz, nstates),
                      klag, f_tab, e_tab, a_t, d_skip, dbsz, dseq, 1)
    w_glu = ssm_w_glu[0].astype(BF16)
    xp = _glu(xp, yp, w_glu)
    xs = _glu(xs, ys, w_glu)
    (y_prompt,) = _ffn(xp, row(norm_ffn[1]), *ffn_w[1], row(norm_final), False)
    (y_sample,) = _ffn(xs, row(norm_ffn[1]), *ffn_w[1], row(norm_final), False)

    state_shape = state_ssm_re.shape[2:]
    return (y_prompt.reshape(bsz, seq, D_MODEL), y_sample.reshape(dbsz, dseq, D_MODEL),
            kp.reshape(1, bsz, seq, heads, HEAD_DIM), vp.reshape(1, bsz, seq, heads, HEAD_DIM),
            ks.reshape(1, dbsz, dseq, heads, HEAD_DIM), vs.reshape(1, dbsz, dseq, heads, HEAD_DIM),
            gs.reshape(1, dbsz, dseq, SGU_WIDTH),
            rp.reshape((1, bsz) + state_shape), ip.reshape((1, bsz) + state_shape),
            rs.reshape((1, dbsz) + state_shape), is_.reshape((1, dbsz) + state_shape))
```

```python
import functools
import math

import jax
import jax.numpy as jnp
from jax import lax
from jax.experimental import pallas as pl
from jax.experimental.pallas import tpu as pltpu

F32 = jnp.float32
BF16 = jnp.bfloat16

LANES = 128
VMEM_LIMIT = 56 * 1024 * 1024

D_MODEL = 1024
HEAD_DIM = 64
SB_WIDTH = 512
SGU_WIDTH = 512
SGU_GROUPS = 8
SGU_CHUNK = 128
SSM_GROUP = 16
SSM_STATE = 64
FFN_HIDDEN = 2816
RMS_EPS = 1e-6

TOKEN_TILE = 512
SB_BLOCK = 128
SB_PAIRS = SB_WIDTH // LANES
SB_CUTOFF = -104.0
S5_T = 16
S5_GROUPS_PER_TILE = LANES // SSM_GROUP
S5_TILE_STATES = S5_GROUPS_PER_TILE * SSM_STATE


def _rmsnorm(x, g):
    return x * lax.rsqrt(jnp.mean(x * x, axis=-1, keepdims=True) + RMS_EPS) * g


def _gelu(x):
    return 0.5 * x * (1.0 + jnp.tanh(math.sqrt(2.0 / math.pi) * (x + 0.044715 * (x * x * x))))


def _sigmoid(x):
    return 1.0 / (1.0 + jnp.exp(-x))


def _dot(a, b):
    return jnp.dot(a, b, preferred_element_type=F32)


def _dot_nt(a, b):
    return lax.dot_general(a, b, (((1,), (1,)), ((), ())), preferred_element_type=F32)


def _params(*sem):
    return pltpu.CompilerParams(dimension_semantics=sem, vmem_limit_bytes=VMEM_LIMIT)


def _resident(shape):
    nd = len(shape)
    return pl.BlockSpec(shape, lambda *_: (0,) * nd, pipeline_mode=pl.Buffered(1))


def _proj_kernel(x_ref, g_ref, w_ref, q_ref, k_ref, v_ref, kb_ref, vb_ref, u_ref, gv_ref):
    h = _rmsnorm(x_ref[...], g_ref[...]).astype(BF16)
    col = lambda i: _dot(h, w_ref[:, i * SB_WIDTH:(i + 1) * SB_WIDTH])
    q_ref[...] = (col(0) * (HEAD_DIM ** -0.5)).astype(BF16)
    k = col(1)
    k_ref[...] = k
    kb_ref[...] = k.astype(BF16)
    v = col(2)
    v_ref[...] = v
    vb_ref[...] = v.astype(BF16)
    u_ref[...] = _gelu(col(3))
    gv_ref[...] = _gelu(col(4))


def _proj(x, g, w_bf16):
    m = x.shape[0]
    tile = lambda w: pl.BlockSpec((TOKEN_TILE, w), lambda i: (i, 0))
    return pl.pallas_call(
        _proj_kernel,
        grid=(m // TOKEN_TILE,),
        in_specs=[tile(D_MODEL), _resident((1, D_MODEL)), _resident(w_bf16.shape)],
        out_specs=[tile(SB_WIDTH)] * 7,
        out_shape=[jax.ShapeDtypeStruct((m, SB_WIDTH), dt) for dt in (BF16, F32, F32, BF16, BF16, F32, F32)],
        compiler_params=_params("parallel"),
        name="proj",
    )(x, g, w_bf16)


def _suffix_and_total():
    j = lax.broadcasted_iota(jnp.int32, (SB_BLOCK, 2 * SB_BLOCK), 0)
    s = lax.broadcasted_iota(jnp.int32, (SB_BLOCK, 2 * SB_BLOCK), 1)
    return jnp.where(jnp.logical_or(j > s, s >= SB_BLOCK), 1.0, 0.0).astype(BF16)


def _sb_all_heads(q_ref, o_ref, new_k, new_v, old_k_ref, old_v_ref, first_block):
    nq = q_ref.shape[0]
    heads = 2 * SB_PAIRS
    lane = lax.broadcasted_iota(jnp.int32, (nq, LANES), 1)
    q_pairs = []
    for p in range(SB_PAIRS):
        q = q_ref[:, p * LANES:(p + 1) * LANES]
        zero = jnp.zeros_like(q)
        q_pairs.append(jnp.concatenate([jnp.where(lane < HEAD_DIM, q, zero),
                                        jnp.where(lane < HEAD_DIM, zero, q)], axis=0))
    weights = _suffix_and_total()
    row = lax.broadcasted_iota(jnp.int32, (heads * nq, SB_BLOCK), 0)
    col = lax.broadcasted_iota(jnp.int32, (heads * nq, SB_BLOCK), 1)
    causal = col < (row & (nq - 1))

    def block(kblk, vblk, c, mask, first):
        z = jnp.concatenate([_dot_nt(q_pairs[p], kblk[:, p * LANES:(p + 1) * LANES])
                             for p in range(SB_PAIRS)], axis=0)
        log_beta = jnp.minimum(z, 0.0) - jnp.log(1.0 + jnp.exp(-jnp.abs(z)))
        log_stay = log_beta - z
        if mask is not None:
            log_stay = jnp.where(mask, log_stay, 0.0)
        hi = log_stay.astype(BF16)
        lo = (log_stay - hi.astype(F32)).astype(BF16)
        sums = _dot(jnp.concatenate([hi, lo], axis=0), weights)
        sums = sums[:heads * nq] + sums[heads * nq:]
        w = jnp.exp(log_beta + sums[:, :SB_BLOCK] + c)
        if mask is not None:
            w = jnp.where(mask, w, 0.0)
        w = w.astype(BF16)
        for p in range(SB_PAIRS):
            cols = slice(p * LANES, (p + 1) * LANES)
            pv = _dot(w[2 * p * nq:(2 * p + 2) * nq], vblk[:, cols])
            merged = jnp.where(lane < HEAD_DIM, pv[:nq], pv[nq:])
            o_ref[:, cols] = merged if first else o_ref[:, cols] + merged
        return c + sums[:, SB_BLOCK:]

    c = block(new_k, new_v, jnp.zeros((heads * nq, SB_BLOCK), F32), causal, True)

    def cond(state):
        kb, c = state
        return jnp.logical_and(kb >= 0, jnp.max(c) > SB_CUTOFF)

    def body(state):
        kb, c = state
        rows = pl.ds(pl.multiple_of(kb * SB_BLOCK, SB_BLOCK), SB_BLOCK)
        return kb - 1, block(old_k_ref[rows, :].astype(BF16), old_v_ref[rows, :].astype(BF16), c, None, False)

    lax.while_loop(cond, body, (first_block, c))


def _sb_prompt_kernel(q_ref, k_ref, v_ref, o_ref):
    qb = pl.program_id(1)
    rows = pl.ds(pl.multiple_of(qb * SB_BLOCK, SB_BLOCK), SB_BLOCK)
    _sb_all_heads(q_ref, o_ref, k_ref[rows, :], v_ref[rows, :], k_ref, v_ref, qb - 1)


def _sb_prompt(q, k, v, bsz, seq):
    nq = seq // SB_BLOCK
    qspec = pl.BlockSpec((SB_BLOCK, SB_WIDTH), lambda b, i: (b * nq + i, 0))
    kvspec = pl.BlockSpec((seq, SB_WIDTH), lambda b, i: (b, 0), pipeline_mode=pl.Buffered(1))
    return pl.pallas_call(
        _sb_prompt_kernel,
        grid=(bsz, nq),
        in_specs=[qspec, kvspec, kvspec],
        out_specs=qspec,
        out_shape=jax.ShapeDtypeStruct((bsz * seq, SB_WIDTH), F32),
        compiler_params=_params("parallel", "arbitrary"),
        name="sb_prompt",
    )(q, k, v)


def _sb_sample_kernel(q_ref, k_ref, v_ref, ck_ref, cv_ref, o_ref):
    last = ck_ref.shape[0] // SB_BLOCK - 1
    pad = jnp.zeros((SB_BLOCK - k_ref.shape[0], SB_WIDTH), BF16)
    new_k = jnp.concatenate([k_ref[...], pad], axis=0)
    new_v = jnp.concatenate([v_ref[...], pad], axis=0)
    _sb_all_heads(q_ref, o_ref, new_k, new_v, ck_ref, cv_ref, jnp.int32(last))


def _sb_sample(q, k, v, cache_k, cache_v, bsz, n):
    past = cache_k.shape[1]
    new = pl.BlockSpec((n, SB_WIDTH), lambda b: (b, 0))
    old = pl.BlockSpec((None, past, SB_WIDTH), lambda b: (b, 0, 0))
    return pl.pallas_call(
        _sb_sample_kernel,
        grid=(bsz,),
        in_specs=[new, new, new, old, old],
        out_specs=new,
        out_shape=jax.ShapeDtypeStruct((bsz * n, SB_WIDTH), F32),
        compiler_params=_params("parallel"),
        name="sb_sample",
    )(q, k, v, cache_k, cache_v)


def _mix_out_kernel(x_ref, att_ref, u_ref, gv_ref, ws_ref, bs_ref, wo_ref, o_ref, sg_ref):
    chunk = ws_ref.shape[1]
    t = lax.broadcasted_iota(jnp.int32, (chunk, chunk), 0)
    s = lax.broadcasted_iota(jnp.int32, (chunk, chunk), 1)
    tri = s <= t
    ws = [jnp.where(tri, ws_ref[g], 0.0).astype(BF16) for g in range(SGU_GROUPS)]
    lane = lax.broadcasted_iota(jnp.int32, (chunk, LANES), 1)
    group_dim = SGU_WIDTH // SGU_GROUPS
    for c in range(TOKEN_TILE // chunk):
        rows = slice(c * chunk, (c + 1) * chunk)
        for p in range(SGU_WIDTH // LANES):
            cols = slice(p * LANES, (p + 1) * LANES)
            gv = gv_ref[rows, cols].astype(BF16)
            mixed = jnp.where(lane < group_dim, _dot(ws[2 * p], gv), _dot(ws[2 * p + 1], gv)) + bs_ref[:, cols]
            sg_ref[rows, cols] = (u_ref[rows, cols] * mixed).astype(BF16)
    o_ref[...] = (x_ref[...] + _dot(att_ref[...].astype(BF16), wo_ref[:SB_WIDTH, :])
                  + _dot(sg_ref[...], wo_ref[SB_WIDTH:, :]))


def _mix_out(x, att, u, gv, ws, bs_rows, wo_bf16):
    m = x.shape[0]
    tile = lambda w: pl.BlockSpec((TOKEN_TILE, w), lambda i: (i, 0))
    return pl.pallas_call(
        _mix_out_kernel,
        grid=(m // TOKEN_TILE,),
        in_specs=[tile(D_MODEL), tile(SB_WIDTH), tile(SGU_WIDTH), tile(SGU_WIDTH),
                  _resident(ws.shape), _resident(bs_rows.shape), _resident(wo_bf16.shape)],
        out_specs=tile(D_MODEL),
        out_shape=jax.ShapeDtypeStruct((m, D_MODEL), F32),
        scratch_shapes=[pltpu.VMEM((TOKEN_TILE, SGU_WIDTH), BF16)],
        compiler_params=_params("parallel"),
        name="mix_out",
    )(x, att, u, gv, ws, bs_rows, wo_bf16)


FFN_CHUNK = FFN_HIDDEN // 2


def _ffn_kernel(x_ref, g_ref, wg_ref, wu_ref, wd_ref, gn_ref, *o_refs):
    x = x_ref[...]
    h = _rmsnorm(x, g_ref[...]).astype(BF16)
    acc = x
    for c in range(FFN_HIDDEN // FFN_CHUNK):
        cols = slice(c * FFN_CHUNK, (c + 1) * FFN_CHUNK)
        gate = _dot(h, wg_ref[:, cols])
        up = _dot(h, wu_ref[:, cols])
        act = (gate * _sigmoid(gate) * up).astype(BF16)
        acc = acc + _dot(act, wd_ref[cols, :])
    if len(o_refs) == 2:
        o_refs[0][...] = acc
    o_refs[-1][...] = _rmsnorm(acc, gn_ref[...])


def _ffn(x, g, wg, wu, wd, g_next, emit_x):
    m = x.shape[0]
    tile = pl.BlockSpec((TOKEN_TILE, D_MODEL), lambda i: (i, 0))
    n_out = 2 if emit_x else 1
    return pl.pallas_call(
        _ffn_kernel,
        grid=(m // TOKEN_TILE,),
        in_specs=[tile, _resident((1, D_MODEL)), _resident(wg.shape), _resident(wu.shape),
                  _resident(wd.shape), _resident((1, D_MODEL))],
        out_specs=[tile] * n_out,
        out_shape=[jax.ShapeDtypeStruct((m, D_MODEL), F32)] * n_out,
        compiler_params=_params("parallel"),
        name="ffn",
    )(x, g, wg, wu, wd, g_next)


def _s5_tables_kernel(lr_ref, li_ref, ls_ref, bre_ref, bim_ref, cre_ref, cim_ref,
                      f_ref, et_ref, klag_ref, at_ref, f32_ref):
    half = S5_TILE_STATES
    lr, li = lr_ref[...], li_ref[...]
    step = jnp.exp(ls_ref[...])
    mag, ang = jnp.exp(lr * step), li * step
    ar, ai = mag * jnp.cos(ang), mag * jnp.sin(ang)
    den = lr * lr + li * li
    nr = ar - 1.0
    co_re = (nr * lr + ai * li) / den
    co_im = (ai * lr - nr * li) / den

    row_group = lax.broadcasted_iota(jnp.int32, (LANES, half), 0) // SSM_GROUP
    col_group = lax.broadcasted_iota(jnp.int32, (LANES, half), 1) // SSM_STATE

    def block_diag(ref):
        return jnp.where(row_group == col_group, jnp.concatenate([ref[...]] * S5_GROUPS_PER_TILE, axis=1), 0.0)

    b_re, b_im, c_re, c_im = (block_diag(r) for r in (bre_ref, bim_ref, cre_ref, cim_ref))
    g_re = co_re * b_re - co_im * b_im
    g_im = co_re * b_im + co_im * b_re
    p_re, p_im = ar, ai
    for tau in range(S5_T):
        rows = slice((S5_T - 1 - tau) * LANES, (S5_T - tau) * LANES)
        f32_ref[rows, :half] = g_re
        f32_ref[rows, half:] = g_im
        rows = slice(tau * LANES, (tau + 1) * LANES)
        et_ref[rows, :half] = (c_re * p_re - c_im * p_im).astype(BF16)
        et_ref[rows, half:] = (-c_re * p_im - c_im * p_re).astype(BF16)
        if tau < S5_T - 1:
            g_re, g_im = g_re * ar - g_im * ai, g_re * ai + g_im * ar
            p_re, p_im = p_re * ar - p_im * ai, p_re * ai + p_im * ar
    at_ref[0:1, :] = jnp.concatenate([p_re, p_re], axis=1)
    at_ref[1:2, :] = jnp.concatenate([-p_im, p_im], axis=1)

    f_all = f32_ref[...]
    f_hi = f_all.astype(BF16)
    f_ref[...] = f_hi
    f_lo = (f_all - f_hi.astype(F32)).astype(BF16)
    cc = jnp.concatenate([c_re, -c_im], axis=1)
    c_hi = cc.astype(BF16)
    c_lo = (cc - c_hi.astype(F32)).astype(BF16)
    k_all = _dot_nt(f_hi, c_hi) + _dot_nt(f_hi, c_lo) + _dot_nt(f_lo, c_hi)
    for tau in range(S5_T):
        rows = slice((S5_T - 1 - tau) * LANES, (S5_T - tau) * LANES)
        klag_ref[tau] = k_all[rows, :].astype(BF16)


def _s5_tables(lam_re, lam_im, log_step, b_re, b_im, c_re, c_im):
    groups, nstate = lam_re.shape
    ntile = groups // S5_GROUPS_PER_TILE
    half = S5_TILE_STATES
    vec = lambda a: a.reshape(ntile, 1, half)
    mat = lambda a: a.reshape(ntile, LANES, nstate)
    args = (vec(lam_re), vec(lam_im), vec(jnp.repeat(log_step, nstate)),
            mat(jnp.swapaxes(b_re, 1, 2)), mat(jnp.swapaxes(b_im, 1, 2)), mat(c_re), mat(c_im))
    per_tile = lambda shape: pl.BlockSpec((None,) + shape, lambda k: (k,) + (0,) * len(shape))
    return pl.pallas_call(
        _s5_tables_kernel,
        grid=(ntile,),
        in_specs=[per_tile((1, half))] * 3 + [per_tile((LANES, nstate))] * 4,
        out_specs=[per_tile((S5_T * LANES, 2 * half)), per_tile((S5_T * LANES, 2 * half)),
                   per_tile((S5_T, LANES, LANES)), per_tile((2, 2 * half))],
        out_shape=[jax.ShapeDtypeStruct((ntile, S5_T * LANES, 2 * half), BF16),
                   jax.ShapeDtypeStruct((ntile, S5_T * LANES, 2 * half), BF16),
                   jax.ShapeDtypeStruct((ntile, S5_T, LANES, LANES), BF16),
                   jax.ShapeDtypeStruct((ntile, 2, 2 * half), F32)],
        scratch_shapes=[pltpu.VMEM((S5_T * LANES, 2 * half), F32)],
        compiler_params=_params("parallel"),
        name="s5_tables",
    )(*args)


def _s5_kernel(h_ref, re0_ref, im0_ref, klag_ref, f_ref, et_ref, at_ref, d_ref,
               y_ref, reo_ref, imo_ref, wint_ref, xend_ref, sprev_ref, s_ref, *, nseq, seq_rows):
    first_of_tile = jnp.logical_and(pl.program_id(1) == 0, pl.program_id(2) == 0)
    nchunk = seq_rows // S5_T
    half = S5_TILE_STATES

    @pl.when(first_of_tile)
    def _():
        wint_ref[...] = jnp.zeros_like(wint_ref)
        for j in range(S5_T):
            for t in range(j, S5_T):
                wint_ref[j * LANES:(j + 1) * LANES, t * LANES:(t + 1) * LANES] = klag_ref[t - j]

    @pl.when(pl.program_id(2) == 0)
    def _():
        s_ref[:, :half] = re0_ref[...]
        s_ref[:, half:] = im0_ref[...]

    def slab(t):
        if nseq == 1:
            return [pl.ds(t, nchunk, stride=S5_T)]
        return [pl.ds(n * S5_T + t, nseq, stride=seq_rows) for n in range(nchunk)]

    u = []
    for t in range(S5_T):
        parts = [h_ref[rows, :] for rows in slab(t)]
        u.append(parts[0] if len(parts) == 1 else jnp.concatenate(parts, axis=0))
    uc = jnp.concatenate([ut.astype(BF16) for ut in u], axis=1)

    xend_ref[...] = _dot(uc, f_ref[...])

    a1 = at_ref[0:1, :]
    a2 = at_ref[1:2, :]

    def step(n, s):
        rows = pl.ds(n * nseq, nseq)
        sprev_ref[rows, :] = s
        swapped = jnp.concatenate([s[:, half:], s[:, :half]], axis=1)
        return a1 * s + a2 * swapped + xend_ref[rows, :]

    s_fin = lax.fori_loop(0, nchunk, step, s_ref[...])
    s_ref[...] = s_fin

    @pl.when(pl.program_id(2) == pl.num_programs(2) - 1)
    def _():
        reo_ref[...] = s_fin[:, :half]
        imo_ref[...] = s_fin[:, half:]

    sprev = sprev_ref[...].astype(BF16)
    d = d_ref[...]
    pair = 2 * LANES
    for cp in range(S5_T // 2):
        depth = (2 * cp + 2) * LANES
        cols = slice(cp * pair, (cp + 1) * pair)
        y2 = _dot(uc[:, :depth], wint_ref[:depth, cols]) + _dot_nt(sprev, et_ref[cols, :])
        for i in range(2):
            t = 2 * cp + i
            yt = _gelu(y2[:, i * LANES:(i + 1) * LANES] + d * u[t])
            off = 0
            for rows in slab(t):
                y_ref[rows, :] = yt[off:off + rows.size, :]
                off += rows.size


def _s5(h, re0, im0, klag, f_tab, et_tab, a_t, d_skip, nseq, seq_rows, blocks_per_seq):
    tokens = h.shape[0]
    nb = re0.shape[0]
    ntile = D_MODEL // LANES
    half = S5_TILE_STATES
    rows = nseq * seq_rows
    nrow = rows // S5_T
    hspec = pl.BlockSpec((rows, LANES), lambda k, b, j: (b * blocks_per_seq + j, k))
    sspec = pl.BlockSpec((None, nseq, half), lambda k, b, j: (b, 0, k))
    per_tile = lambda shape: pl.BlockSpec((None,) + shape, lambda k, b, j: (k,) + (0,) * len(shape))
    kern = functools.partial(_s5_kernel, nseq=nseq, seq_rows=seq_rows)
    return pl.pallas_call(
        kern,
        grid=(ntile, nb, blocks_per_seq),
        in_specs=[hspec, sspec, sspec,
                  per_tile((S5_T, LANES, LANES)), per_tile((S5_T * LANES, 2 * half)),
                  per_tile((S5_T * LANES, 2 * half)), per_tile((2, 2 * half)),
                  pl.BlockSpec((1, LANES), lambda k, b, j: (0, k))],
        out_specs=[hspec, sspec, sspec],
        out_shape=[jax.ShapeDtypeStruct((tokens, D_MODEL), F32),
                   jax.ShapeDtypeStruct(re0.shape, F32), jax.ShapeDtypeStruct(im0.shape, F32)],
        scratch_shapes=[pltpu.VMEM((S5_T * LANES, S5_T * LANES), BF16),
                        pltpu.VMEM((nrow, 2 * half), F32),
                        pltpu.VMEM((nrow, 2 * half), F32),
                        pltpu.VMEM((nseq, 2 * half), F32)],
        compiler_params=_params("arbitrary", "arbitrary", "arbitrary"),
        name="s5",
    )(h, re0, im0, klag, f_tab, et_tab, a_t, d_skip)


def _glu_kernel(x_ref, y_ref, w_ref, o_ref):
    y = y_ref[...].astype(BF16)
    ga = _dot(y, w_ref[:, :D_MODEL])
    gb = _dot(y, w_ref[:, D_MODEL:])
    o_ref[...] = x_ref[...] + ga * _sigmoid(gb)


def _glu(x, y, w_bf16):
    m = x.shape[0]
    tile = pl.BlockSpec((TOKEN_TILE, D_MODEL), lambda i: (i, 0))
    return pl.pallas_call(
        _glu_kernel,
        grid=(m // TOKEN_TILE,),
        in_specs=[tile, tile, _resident(w_bf16.shape)],
        out_specs=tile,
        out_shape=jax.ShapeDtypeStruct((m, D_MODEL), F32),
        compiler_params=_params("parallel"),
        name="glu",
    )(x, y, w_bf16)


def kernel(x_prompt, x_sample, cache_sb_k, cache_sb_v, state_ssm_re, state_ssm_im, norm_mix, norm_ffn,
           norm_final, ab_w_in, sgu_w, sgu_b, ab_w_out, ssm_lam_re, ssm_lam_im, ssm_log_step, ssm_b_re,
           ssm_b_im, ssm_c_re, ssm_c_im, ssm_d, ssm_w_glu, ffn_w_gate, ffn_w_up, ffn_w_down):
    bsz, seq, _ = x_prompt.shape
    dbsz, dseq, _ = x_sample.shape
    past = cache_sb_k.shape[2]
    heads = SB_WIDTH // HEAD_DIM
    row = lambda v: v.reshape(1, -1)

    xp = x_prompt.reshape(bsz * seq, D_MODEL)
    xs = x_sample.reshape(dbsz * dseq, D_MODEL)

    w_in = ab_w_in[0].astype(BF16)
    w_out = ab_w_out[0].astype(BF16)
    bs_rows = jnp.repeat(sgu_b[0].T, SGU_WIDTH // SGU_GROUPS, axis=1)
    g_mix0 = row(norm_mix[0])

    qp, kp, vp, kpb, vpb, up, gp = _proj(xp, g_mix0, w_in)
    qs, ks, vs, ksb, vsb, us, gs = _proj(xs, g_mix0, w_in)
    att_p = _sb_prompt(qp, kpb, vpb, bsz, seq)
    att_s = _sb_sample(qs, ksb, vsb, cache_sb_k[0].reshape(dbsz, past, SB_WIDTH),
                       cache_sb_v[0].reshape(dbsz, past, SB_WIDTH), dbsz, dseq)
    xp = _mix_out(xp, att_p, up, gp, sgu_w[0], bs_rows, w_out)
    xs = _mix_out(xs, att_s, us, gs, sgu_w[0][:, :dseq, :dseq], bs_rows[:dseq], w_out)

    ffn_w = [(ffn_w_gate[l].astype(BF16), ffn_w_up[l].astype(BF16), ffn_w_down[l].astype(BF16)) for l in range(2)]
    g_mix1 = row(norm_mix[1])
    xp, hp = _ffn(xp, row(norm_ffn[0]), *ffn_w[0], g_mix1, True)
    xs, hs = _ffn(xs, row(norm_ffn[0]), *ffn_w[0], g_mix1, True)

    f_tab, et_tab, klag, a_t = _s5_tables(ssm_lam_re[0], ssm_lam_im[0], ssm_log_step[0], ssm_b_re[0],
                                          ssm_b_im[0], ssm_c_re[0], ssm_c_im[0])
    d_skip = row(ssm_d[0])
    nstates = state_ssm_re.shape[2] * state_ssm_re.shape[3]
    zeros = jnp.zeros((bsz, 1, nstates), F32)
    prompt_block = 4096
    yp, rp, ip = _s5(hp, zeros, zeros, klag, f_tab, et_tab, a_t, d_skip, 1, prompt_block, seq // prompt_block)
    ys, rs, is_ = _s5(hs, state_ssm_re[0].reshape(1, dbsz, nstates), state_ssm_im[0].reshape(1, dbsz, nstates),
                      klag, f_tab, et_tab, a_t, d_skip, dbsz, dseq, 1)
    w_glu = ssm_w_glu[0].astype(BF16)
    xp = _glu(xp, yp, w_glu)
    xs = _glu(xs, ys, w_glu)
    (y_prompt,) = _ffn(xp, row(norm_ffn[1]), *ffn_w[1], row(norm_final), False)
    (y_sample,) = _ffn(xs, row(norm_ffn[1]), *ffn_w[1], row(norm_final), False)

    state_shape = state_ssm_re.shape[2:]
    return (y_prompt.reshape(bsz, seq, D_MODEL), y_sample.reshape(dbsz, dseq, D_MODEL),
            kp.reshape(1, bsz, seq, heads, HEAD_DIM), vp.reshape(1, bsz, seq, heads, HEAD_DIM),
            ks.reshape(1, dbsz, dseq, heads, HEAD_DIM), vs.reshape(1, dbsz, dseq, heads, HEAD_DIM),
            gs.reshape(1, dbsz, dseq, SGU_WIDTH),
            rp.reshape((1, bsz) + state_shape), ip.reshape((1, bsz) + state_shape),
            rs.reshape((1, dbsz) + state_shape), is_.reshape((1, dbsz) + state_shape))
```

```python
import functools
import math

import jax
import jax.numpy as jnp
from jax import lax
from jax.experimental import pallas as pl
from jax.experimental.pallas import tpu as pltpu

F32 = jnp.float32
BF16 = jnp.bfloat16

LANES = 128
VMEM_LIMIT = 56 * 1024 * 1024

D_MODEL = 1024
HEAD_DIM = 64
SB_WIDTH = 512
SGU_WIDTH = 512
SGU_GROUPS = 8
SGU_CHUNK = 128
SSM_GROUP = 16
SSM_STATE = 64
FFN_HIDDEN = 2816
RMS_EPS = 1e-6

TOKEN_TILE = 512
SB_BLOCK = 128
SB_HEADS = SB_WIDTH // HEAD_DIM
SB_PAIRS = SB_WIDTH // LANES
SB_CUTOFF = -104.0
S5_T = 16
S5_GROUPS_PER_TILE = LANES // SSM_GROUP
S5_TILE_STATES = S5_GROUPS_PER_TILE * SSM_STATE


def _rmsnorm(x, g):
    return x * lax.rsqrt(jnp.mean(x * x, axis=-1, keepdims=True) + RMS_EPS) * g


def _gelu(x):
    return 0.5 * x * (1.0 + jnp.tanh(math.sqrt(2.0 / math.pi) * (x + 0.044715 * (x * x * x))))


def _sigmoid(x):
    return 1.0 / (1.0 + jnp.exp(-x))


def _dot(a, b):
    return jnp.dot(a, b, preferred_element_type=F32)


def _dot_nt(a, b):
    return lax.dot_general(a, b, (((1,), (1,)), ((), ())), preferred_element_type=F32)


def _params(*sem):
    return pltpu.CompilerParams(dimension_semantics=sem, vmem_limit_bytes=VMEM_LIMIT)


def _resident(shape):
    nd = len(shape)
    return pl.BlockSpec(shape, lambda *_: (0,) * nd, pipeline_mode=pl.Buffered(1))


def _proj_kernel(x_ref, g_ref, w_ref, q_ref, k_ref, v_ref, kb_ref, vb_ref, u_ref, gv_ref):
    h = _rmsnorm(x_ref[...], g_ref[...]).astype(BF16)
    col = lambda i: _dot(h, w_ref[:, i * SB_WIDTH:(i + 1) * SB_WIDTH])
    q_ref[...] = (col(0) * (HEAD_DIM ** -0.5)).astype(BF16)
    for f32_ref, bf16_ref, z in ((k_ref, kb_ref, col(1)), (v_ref, vb_ref, col(2))):
        bf16_ref[...] = z.astype(BF16)
        f32_ref[...] = z.reshape(f32_ref.shape)
    u_ref[...] = _gelu(col(3))
    gv_ref[...] = _gelu(col(4))


def _proj(x, g, w_bf16):
    m = x.shape[0]
    tile = lambda w: pl.BlockSpec((TOKEN_TILE, w), lambda i: (i, 0))
    heads = pl.BlockSpec((TOKEN_TILE, SB_HEADS, HEAD_DIM), lambda i: (i, 0, 0))
    flat = lambda dt: jax.ShapeDtypeStruct((m, SB_WIDTH), dt)
    by_head = jax.ShapeDtypeStruct((m, SB_HEADS, HEAD_DIM), F32)
    return pl.pallas_call(
        _proj_kernel,
        grid=(m // TOKEN_TILE,),
        in_specs=[tile(D_MODEL), _resident((1, D_MODEL)), _resident(w_bf16.shape)],
        out_specs=[tile(SB_WIDTH), heads, heads] + [tile(SB_WIDTH)] * 4,
        out_shape=[flat(BF16), by_head, by_head, flat(BF16), flat(BF16), flat(F32), flat(F32)],
        compiler_params=_params("parallel"),
        name="proj",
    )(x, g, w_bf16)


def _suffix_and_total():
    j = lax.broadcasted_iota(jnp.int32, (SB_BLOCK, 2 * SB_BLOCK), 0)
    s = lax.broadcasted_iota(jnp.int32, (SB_BLOCK, 2 * SB_BLOCK), 1)
    return jnp.where(jnp.logical_or(j > s, s >= SB_BLOCK), 1.0, 0.0).astype(BF16)


def _sb_weights(z, c, sums_mat, mask):
    rows = z.shape[0]
    log_beta = jnp.minimum(z, 0.0) - jnp.log(1.0 + jnp.exp(-jnp.abs(z)))
    log_stay = log_beta - z
    if mask is not None:
        log_stay = jnp.where(mask, log_stay, 0.0)
    hi = log_stay.astype(BF16)
    lo = (log_stay - hi.astype(F32)).astype(BF16)
    sums = _dot(jnp.concatenate([hi, lo], axis=0), sums_mat)
    sums = sums[:rows] + sums[rows:]
    w = jnp.exp(log_beta + sums[:, :SB_BLOCK] + c)
    if mask is not None:
        w = jnp.where(mask, w, 0.0)
    return w.astype(BF16), c + sums[:, SB_BLOCK:]


def _sb_alive(c):
    return jnp.max(c) > SB_CUTOFF


def _sb_pair_block(q_ref, o_ref, sums_mat):
    nq = q_ref.shape[0]
    lane = lax.broadcasted_iota(jnp.int32, (nq, LANES), 1)
    q_pairs = []
    for p in range(SB_PAIRS):
        q = q_ref[:, p * LANES:(p + 1) * LANES]
        zero = jnp.zeros_like(q)
        q_pairs.append(jnp.concatenate([jnp.where(lane < HEAD_DIM, q, zero),
                                        jnp.where(lane < HEAD_DIM, zero, q)], axis=0))

    def block(kblk, vblk, c, mask, first):
        z = jnp.concatenate([_dot_nt(q_pairs[p], kblk[:, p * LANES:(p + 1) * LANES])
                             for p in range(SB_PAIRS)], axis=0)
        w, c = _sb_weights(z, c, sums_mat, mask)
        for p in range(SB_PAIRS):
            cols = slice(p * LANES, (p + 1) * LANES)
            pv = _dot(w[2 * p * nq:(2 * p + 2) * nq], vblk[:, cols])
            merged = jnp.where(lane < HEAD_DIM, pv[:nq], pv[nq:])
            o_ref[:, cols] = merged if first else o_ref[:, cols] + merged
        return c

    return block


def _sb_causal(nq):
    row = lax.broadcasted_iota(jnp.int32, (SB_HEADS * nq, SB_BLOCK), 0)
    col = lax.broadcasted_iota(jnp.int32, (SB_HEADS * nq, SB_BLOCK), 1)
    return col < (row & (nq - 1))


def _sb_prompt_kernel(q_ref, k_ref, v_ref, o_ref):
    qb = pl.program_id(1)
    block = _sb_pair_block(q_ref, o_ref, _suffix_and_total())
    rows = pl.ds(pl.multiple_of(qb * SB_BLOCK, SB_BLOCK), SB_BLOCK)
    c = block(k_ref[rows, :], v_ref[rows, :], jnp.zeros((SB_HEADS * SB_BLOCK, SB_BLOCK), F32),
              _sb_causal(SB_BLOCK), True)

    def cond(state):
        kb, c = state
        return jnp.logical_and(kb >= 0, _sb_alive(c))

    def body(state):
        kb, c = state
        rows = pl.ds(pl.multiple_of(kb * SB_BLOCK, SB_BLOCK), SB_BLOCK)
        return kb - 1, block(k_ref[rows, :], v_ref[rows, :], c, None, False)

    lax.while_loop(cond, body, (qb - 1, c))


def _sb_prompt(q, k, v, bsz, seq):
    nq = seq // SB_BLOCK
    qspec = pl.BlockSpec((SB_BLOCK, SB_WIDTH), lambda b, i: (b * nq + i, 0))
    kvspec = pl.BlockSpec((seq, SB_WIDTH), lambda b, i: (b, 0), pipeline_mode=pl.Buffered(1))
    return pl.pallas_call(
        _sb_prompt_kernel,
        grid=(bsz, nq),
        in_specs=[qspec, kvspec, kvspec],
        out_specs=qspec,
        out_shape=jax.ShapeDtypeStruct((bsz * seq, SB_WIDTH), F32),
        compiler_params=_params("parallel", "arbitrary"),
        name="sb_prompt",
    )(q, k, v)


def _sb_sample_kernel(q_ref, k_ref, v_ref, ck_hbm, cv_hbm, o_ref, kbuf, vbuf, sem):
    b = pl.program_id(0)
    nq = q_ref.shape[0]
    last = ck_hbm.shape[1] // SB_BLOCK - 1

    def fetch(kb, slot):
        rows = pl.ds(kb * SB_BLOCK, SB_BLOCK)
        return (pltpu.make_async_copy(ck_hbm.at[b, rows], kbuf.at[slot], sem.at[0, slot]),
                pltpu.make_async_copy(cv_hbm.at[b, rows], vbuf.at[slot], sem.at[1, slot]))

    for cp in fetch(last, last % 2):
        cp.start()

    sums_mat = _suffix_and_total()
    pad = jnp.zeros((SB_BLOCK - k_ref.shape[0], SB_WIDTH), BF16)
    new_k = jnp.concatenate([k_ref[...], pad], axis=0)
    new_v = jnp.concatenate([v_ref[...], pad], axis=0)
    block = _sb_pair_block(q_ref, o_ref, sums_mat)
    c = block(new_k, new_v, jnp.zeros((SB_HEADS * nq, SB_BLOCK), F32), _sb_causal(nq), True)

    def cond(state):
        kb, c = state
        return jnp.logical_and(kb >= 0, _sb_alive(c))

    def body(state):
        kb, c = state
        slot = kb % 2
        for cp in fetch(kb, slot):
            cp.wait()

        @pl.when(kb >= 1)
        def _():
            for cp in fetch(kb - 1, 1 - slot):
                cp.start()

        kblk = kbuf[slot].reshape(SB_BLOCK, SB_WIDTH).astype(BF16)
        vblk = vbuf[slot].reshape(SB_BLOCK, SB_WIDTH).astype(BF16)
        return kb - 1, block(kblk, vblk, c, None, False)

    kb_end, _ = lax.while_loop(cond, body, (jnp.int32(last), c))

    @pl.when(kb_end >= 0)
    def _():
        for cp in fetch(kb_end, kb_end % 2):
            cp.wait()


def _sb_sample(q, k, v, cache_k, cache_v, bsz, n):
    new = pl.BlockSpec((n, SB_WIDTH), lambda b: (b, 0))
    old = pl.BlockSpec(memory_space=pl.ANY)
    slots = pltpu.VMEM((2, SB_BLOCK, SB_HEADS, HEAD_DIM), F32)
    return pl.pallas_call(
        _sb_sample_kernel,
        grid=(bsz,),
        in_specs=[new, new, new, old, old],
        out_specs=new,
        out_shape=jax.ShapeDtypeStruct((bsz * n, SB_WIDTH), F32),
        scratch_shapes=[slots, slots, pltpu.SemaphoreType.DMA((2, 2))],
        compiler_params=_params("arbitrary"),
        name="sb_sample",
    )(q, k, v, cache_k, cache_v)


def _mix_out_kernel(x_ref, att_ref, u_ref, gv_ref, ws_ref, bs_ref, wo_ref, o_ref, sg_ref):
    chunk = ws_ref.shape[1]
    t = lax.broadcasted_iota(jnp.int32, (chunk, chunk), 0)
    s = lax.broadcasted_iota(jnp.int32, (chunk, chunk), 1)
    tri = s <= t
    ws = [jnp.where(tri, ws_ref[g], 0.0).astype(BF16) for g in range(SGU_GROUPS)]
    lane = lax.broadcasted_iota(jnp.int32, (chunk, LANES), 1)
    group_dim = SGU_WIDTH // SGU_GROUPS
    for c in range(TOKEN_TILE // chunk):
        rows = slice(c * chunk, (c + 1) * chunk)
        for p in range(SGU_WIDTH // LANES):
            cols = slice(p * LANES, (p + 1) * LANES)
            gv = gv_ref[rows, cols].astype(BF16)
            mixed = jnp.where(lane < group_dim, _dot(ws[2 * p], gv), _dot(ws[2 * p + 1], gv)) + bs_ref[:, cols]
            sg_ref[rows, cols] = (u_ref[rows, cols] * mixed).astype(BF16)
    o_ref[...] = (x_ref[...] + _dot(att_ref[...].astype(BF16), wo_ref[:SB_WIDTH, :])
                  + _dot(sg_ref[...], wo_ref[SB_WIDTH:, :]))


def _mix_out(x, att, u, gv, ws, bs_rows, wo_bf16):
    m = x.shape[0]
    tile = lambda w: pl.BlockSpec((TOKEN_TILE, w), lambda i: (i, 0))
    return pl.pallas_call(
        _mix_out_kernel,
        grid=(m // TOKEN_TILE,),
        in_specs=[tile(D_MODEL), tile(SB_WIDTH), tile(SGU_WIDTH), tile(SGU_WIDTH),
                  _resident(ws.shape), _resident(bs_rows.shape), _resident(wo_bf16.shape)],
        out_specs=tile(D_MODEL),
        out_shape=jax.ShapeDtypeStruct((m, D_MODEL), F32),
        scratch_shapes=[pltpu.VMEM((TOKEN_TILE, SGU_WIDTH), BF16)],
        compiler_params=_params("parallel"),
        name="mix_out",
    )(x, att, u, gv, ws, bs_rows, wo_bf16)


FFN_CHUNK = FFN_HIDDEN // 2


def _ffn_kernel(x_ref, g_ref, wg_ref, wu_ref, wd_ref, gn_ref, *o_refs):
    x = x_ref[...]
    h = _rmsnorm(x, g_ref[...]).astype(BF16)
    acc = x
    for c in range(FFN_HIDDEN // FFN_CHUNK):
        cols = slice(c * FFN_CHUNK, (c + 1) * FFN_CHUNK)
        gate = _dot(h, wg_ref[:, cols])
        up = _dot(h, wu_ref[:, cols])
        act = (gate * _sigmoid(gate) * up).astype(BF16)
        acc = acc + _dot(act, wd_ref[cols, :])
    if len(o_refs) == 2:
        o_refs[0][...] = acc
    o_refs[-1][...] = _rmsnorm(acc, gn_ref[...])


def _ffn(x, g, wg, wu, wd, g_next, emit_x):
    m = x.shape[0]
    tile = pl.BlockSpec((TOKEN_TILE, D_MODEL), lambda i: (i, 0))
    n_out = 2 if emit_x else 1
    return pl.pallas_call(
        _ffn_kernel,
        grid=(m // TOKEN_TILE,),
        in_specs=[tile, _resident((1, D_MODEL)), _resident(wg.shape), _resident(wu.shape),
                  _resident(wd.shape), _resident((1, D_MODEL))],
        out_specs=[tile] * n_out,
        out_shape=[jax.ShapeDtypeStruct((m, D_MODEL), F32)] * n_out,
        compiler_params=_params("parallel"),
        name="ffn",
    )(x, g, wg, wu, wd, g_next)


def _s5_tables_kernel(lr_ref, li_ref, ls_ref, bre_ref, bim_ref, cre_ref, cim_ref,
                      f_ref, et_ref, klag_ref, at_ref, f32_ref):
    half = S5_TILE_STATES
    lr, li = lr_ref[...], li_ref[...]
    step = jnp.exp(ls_ref[...])
    mag, ang = jnp.exp(lr * step), li * step
    ar, ai = mag * jnp.cos(ang), mag * jnp.sin(ang)
    den = lr * lr + li * li
    nr = ar - 1.0
    co_re = (nr * lr + ai * li) / den
    co_im = (ai * lr - nr * li) / den

    row_group = lax.broadcasted_iota(jnp.int32, (LANES, half), 0) // SSM_GROUP
    col_group = lax.broadcasted_iota(jnp.int32, (LANES, half), 1) // SSM_STATE

    def block_diag(ref):
        return jnp.where(row_group == col_group, jnp.concatenate([ref[...]] * S5_GROUPS_PER_TILE, axis=1), 0.0)

    b_re, b_im, c_re, c_im = (block_diag(r) for r in (bre_ref, bim_ref, cre_ref, cim_ref))
    g_re = co_re * b_re - co_im * b_im
    g_im = co_re * b_im + co_im * b_re
    p_re, p_im = ar, ai
    for tau in range(S5_T):
        rows = slice((S5_T - 1 - tau) * LANES, (S5_T - tau) * LANES)
        f32_ref[rows, :half] = g_re
        f32_ref[rows, half:] = g_im
        rows = slice(tau * LANES, (tau + 1) * LANES)
        et_ref[rows, :half] = (c_re * p_re - c_im * p_im).astype(BF16)
        et_ref[rows, half:] = (-c_re * p_im - c_im * p_re).astype(BF16)
        if tau < S5_T - 1:
            g_re, g_im = g_re * ar - g_im * ai, g_re * ai + g_im * ar
            p_re, p_im = p_re * ar - p_im * ai, p_re * ai + p_im * ar
    at_ref[0:1, :] = jnp.concatenate([p_re, p_re], axis=1)
    at_ref[1:2, :] = jnp.concatenate([-p_im, p_im], axis=1)

    f_all = f32_ref[...]
    f_hi = f_all.astype(BF16)
    f_ref[...] = f_hi
    f_lo = (f_all - f_hi.astype(F32)).astype(BF16)
    cc = jnp.concatenate([c_re, -c_im], axis=1)
    c_hi = cc.astype(BF16)
    c_lo = (cc - c_hi.astype(F32)).astype(BF16)
    k_all = _dot_nt(f_hi, c_hi) + _dot_nt(f_hi, c_lo) + _dot_nt(f_lo, c_hi)
    for tau in range(S5_T):
        rows = slice((S5_T - 1 - tau) * LANES, (S5_T - tau) * LANES)
        klag_ref[tau] = k_all[rows, :].astype(BF16)


def _s5_tables(lam_re, lam_im, log_step, b_re, b_im, c_re, c_im):
    groups, nstate = lam_re.shape
    ntile = groups // S5_GROUPS_PER_TILE
    half = S5_TILE_STATES
    vec = lambda a: a.reshape(ntile, 1, half)
    mat = lambda a: a.reshape(ntile, LANES, nstate)
    args = (vec(lam_re), vec(lam_im), vec(jnp.repeat(log_step, nstate)),
            mat(jnp.swapaxes(b_re, 1, 2)), mat(jnp.swapaxes(b_im, 1, 2)), mat(c_re), mat(c_im))
    per_tile = lambda shape: pl.BlockSpec((None,) + shape, lambda k: (k,) + (0,) * len(shape))
    return pl.pallas_call(
        _s5_tables_kernel,
        grid=(ntile,),
        in_specs=[per_tile((1, half))] * 3 + [per_tile((LANES, nstate))] * 4,
        out_specs=[per_tile((S5_T * LANES, 2 * half)), per_tile((S5_T * LANES, 2 * half)),
                   per_tile((S5_T, LANES, LANES)), per_tile((2, 2 * half))],
        out_shape=[jax.ShapeDtypeStruct((ntile, S5_T * LANES, 2 * half), BF16),
                   jax.ShapeDtypeStruct((ntile, S5_T * LANES, 2 * half), BF16),
                   jax.ShapeDtypeStruct((ntile, S5_T, LANES, LANES), BF16),
                   jax.ShapeDtypeStruct((ntile, 2, 2 * half), F32)],
        scratch_shapes=[pltpu.VMEM((S5_T * LANES, 2 * half), F32)],
        compiler_params=_params("parallel"),
        name="s5_tables",
    )(*args)


def _s5_kernel(h_ref, re0_ref, im0_ref, klag_ref, f_ref, et_ref, at_ref, d_ref,
               y_ref, reo_ref, imo_ref, wint_ref, xend_ref, sprev_ref, s_ref, *, nseq, seq_rows):
    first_of_tile = jnp.logical_and(pl.program_id(1) == 0, pl.program_id(2) == 0)
    nchunk = seq_rows // S5_T
    half = S5_TILE_STATES

    @pl.when(first_of_tile)
    def _():
        wint_ref[...] = jnp.zeros_like(wint_ref)
        for j in range(S5_T):
            for t in range(j, S5_T):
                wint_ref[j * LANES:(j + 1) * LANES, t * LANES:(t + 1) * LANES] = klag_ref[t - j]

    @pl.when(pl.program_id(2) == 0)
    def _():
        s_ref[:, :half] = re0_ref[...]
        s_ref[:, half:] = im0_ref[...]

    def slab(t):
        if nseq == 1:
            return [pl.ds(t, nchunk, stride=S5_T)]
        return [pl.ds(n * S5_T + t, nseq, stride=seq_rows) for n in range(nchunk)]

    u = []
    for t in range(S5_T):
        parts = [h_ref[rows, :] for rows in slab(t)]
        u.append(parts[0] if len(parts) == 1 else jnp.concatenate(parts, axis=0))
    uc = jnp.concatenate([ut.astype(BF16) for ut in u], axis=1)

    xend_ref[...] = _dot(uc, f_ref[...])

    a1 = at_ref[0:1, :]
    a2 = at_ref[1:2, :]

    def step(n, s):
        rows = pl.ds(n * nseq, nseq)
        sprev_ref[rows, :] = s
        swapped = jnp.concatenate([s[:, half:], s[:, :half]], axis=1)
        return a1 * s + a2 * swapped + xend_ref[rows, :]

    s_fin = lax.fori_loop(0, nchunk, step, s_ref[...])
    s_ref[...] = s_fin

    @pl.when(pl.program_id(2) == pl.num_programs(2) - 1)
    def _():
        reo_ref[...] = s_fin[:, :half]
        imo_ref[...] = s_fin[:, half:]

    sprev = sprev_ref[...].astype(BF16)
    d = d_ref[...]
    pair = 2 * LANES
    for cp in range(S5_T // 2):
        depth = (2 * cp + 2) * LANES
        cols = slice(cp * pair, (cp + 1) * pair)
        y2 = _dot(uc[:, :depth], wint_ref[:depth, cols]) + _dot_nt(sprev, et_ref[cols, :])
        for i in range(2):
            t = 2 * cp + i
            yt = _gelu(y2[:, i * LANES:(i + 1) * LANES] + d * u[t])
            off = 0
            for rows in slab(t):
                y_ref[rows, :] = yt[off:off + rows.size, :]
                off += rows.size


def _s5(h, re0, im0, klag, f_tab, et_tab, a_t, d_skip, nseq, seq_rows, blocks_per_seq):
    tokens = h.shape[0]
    nb = re0.shape[0]
    ntile = D_MODEL // LANES
    half = S5_TILE_STATES
    rows = nseq * seq_rows
    nrow = rows // S5_T
    hspec = pl.BlockSpec((rows, LANES), lambda k, b, j: (b * blocks_per_seq + j, k))
    sspec = pl.BlockSpec((None, nseq, half), lambda k, b, j: (b, 0, k))
    per_tile = lambda shape: pl.BlockSpec((None,) + shape, lambda k, b, j: (k,) + (0,) * len(shape))
    kern = functools.partial(_s5_kernel, nseq=nseq, seq_rows=seq_rows)
    return pl.pallas_call(
        kern,
        grid=(ntile, nb, blocks_per_seq),
        in_specs=[hspec, sspec, sspec,
                  per_tile((S5_T, LANES, LANES)), per_tile((S5_T * LANES, 2 * half)),
                  per_tile((S5_T * LANES, 2 * half)), per_tile((2, 2 * half)),
                  pl.BlockSpec((1, LANES), lambda k, b, j: (0, k))],
        out_specs=[hspec, sspec, sspec],
        out_shape=[jax.ShapeDtypeStruct((tokens, D_MODEL), F32),
                   jax.ShapeDtypeStruct(re0.shape, F32), jax.ShapeDtypeStruct(im0.shape, F32)],
        scratch_shapes=[pltpu.VMEM((S5_T * LANES, S5_T * LANES), BF16),
                        pltpu.VMEM((nrow, 2 * half), F32),
                        pltpu.VMEM((nrow, 2 * half), F32),
                        pltpu.VMEM((nseq, 2 * half), F32)],
        compiler_params=_params("arbitrary", "arbitrary", "arbitrary"),
        name="s5",
    )(h, re0, im0, klag, f_tab, et_tab, a_t, d_skip)


def _glu_kernel(x_ref, y_ref, w_ref, o_ref):
    y = y_ref[...].astype(BF16)
    ga = _dot(y, w_ref[:, :D_MODEL])
    gb = _dot(y, w_ref[:, D_MODEL:])
    o_ref[...] = x_ref[...] + ga * _sigmoid(gb)


def _glu(x, y, w_bf16):
    m = x.shape[0]
    tile = pl.BlockSpec((TOKEN_TILE, D_MODEL), lambda i: (i, 0))
    return pl.pallas_call(
        _glu_kernel,
        grid=(m // TOKEN_TILE,),
        in_specs=[tile, tile, _resident(w_bf16.shape)],
        out_specs=tile,
        out_shape=jax.ShapeDtypeStruct((m, D_MODEL), F32),
        compiler_params=_params("parallel"),
        name="glu",
    )(x, y, w_bf16)


def kernel(x_prompt, x_sample, cache_sb_k, cache_sb_v, state_ssm_re, state_ssm_im, norm_mix, norm_ffn,
           norm_final, ab_w_in, sgu_w, sgu_b, ab_w_out, ssm_lam_re, ssm_lam_im, ssm_log_step, ssm_b_re,
           ssm_b_im, ssm_c_re, ssm_c_im, ssm_d, ssm_w_glu, ffn_w_gate, ffn_w_up, ffn_w_down):
    bsz, seq, _ = x_prompt.shape
    dbsz, dseq, _ = x_sample.shape
    past = cache_sb_k.shape[2]
    heads = SB_WIDTH // HEAD_DIM
    row = lambda v: v.reshape(1, -1)

    xp = x_prompt.reshape(bsz * seq, D_MODEL)
    xs = x_sample.reshape(dbsz * dseq, D_MODEL)

    w_in = ab_w_in[0].astype(BF16)
    w_out = ab_w_out[0].astype(BF16)
    bs_rows = jnp.repeat(sgu_b[0].T, SGU_WIDTH // SGU_GROUPS, axis=1)
    g_mix0 = row(norm_mix[0])

    qp, kp, vp, kpb, vpb, up, gp = _proj(xp, g_mix0, w_in)
    qs, ks, vs, ksb, vsb, us, gs = _proj(xs, g_mix0, w_in)
    att_p = _sb_prompt(qp, kpb, vpb, bsz, seq)
    att_s = _sb_sample(qs, ksb, vsb, cache_sb_k[0], cache_sb_v[0], dbsz, dseq)
    xp = _mix_out(xp, att_p, up, gp, sgu_w[0], bs_rows, w_out)
    xs = _mix_out(xs, att_s, us, gs, sgu_w[0][:, :dseq, :dseq], bs_rows[:dseq], w_out)

    ffn_w = [(ffn_w_gate[l].astype(BF16), ffn_w_up[l].astype(BF16), ffn_w_down[l].astype(BF16)) for l in range(2)]
    g_mix1 = row(norm_mix[1])
    xp, hp = _ffn(xp, row(norm_ffn[0]), *ffn_w[0], g_mix1, True)
    xs, hs = _ffn(xs, row(norm_ffn[0]), *ffn_w[0], g_mix1, True)

    f_tab, et_tab, klag, a_t = _s5_tables(ssm_lam_re[0], ssm_lam_im[0], ssm_log_step[0], ssm_b_re[0],
                                          ssm_b_im[0], ssm_c_re[0], ssm_c_im[0])
    d_skip = row(ssm_d[0])
    nstates = state_ssm_re.shape[2] * state_ssm_re.shape[3]
    zeros = jnp.zeros((bsz, 1, nstates), F32)
    prompt_block = 4096
    yp, rp, ip = _s5(hp, zeros, zeros, klag, f_tab, et_tab, a_t, d_skip, 1, prompt_block, seq // prompt_block)
    ys, rs, is_ = _s5(hs, state_ssm_re[0].reshape(1, dbsz, nstates), state_ssm_im[0].reshape(1, dbsz, nstates),
                      klag, f_tab, et_tab, a_t, d_skip, dbsz, dseq, 1)
    w_glu = ssm_w_glu[0].astype(BF16)
    xp = _glu(xp, yp, w_glu)
    xs = _glu(xs, ys, w_glu)
    (y_prompt,) = _ffn(xp, row(norm_ffn[1]), *ffn_w[1], row(norm_final), False)
    (y_sample,) = _ffn(xs, row(norm_ffn[1]), *ffn_w[1], row(norm_final), False)

    state_shape = state_ssm_re.shape[2:]
    return (y_prompt.reshape(bsz, seq, D_MODEL), y_sample.reshape(dbsz, dseq, D_MODEL),
            kp.reshape(1, bsz, seq, heads, HEAD_DIM), vp.reshape(1, bsz, seq, heads, HEAD_DIM),
            ks.reshape(1, dbsz, dseq, heads, HEAD_DIM), vs.reshape(1, dbsz, dseq, heads, HEAD_DIM),
            gs.reshape(1, dbsz, dseq, SGU_WIDTH),
            rp.reshape((1, bsz) + state_shape), ip.reshape((1, bsz) + state_shape),
            rs.reshape((1, dbsz) + state_shape), is_.reshape((1, dbsz) + state_shape))
```

```python
import functools
import math

import jax
import jax.numpy as jnp
from jax import lax
from jax.experimental import pallas as pl
from jax.experimental.pallas import tpu as pltpu

F32 = jnp.float32
BF16 = jnp.bfloat16

LANES = 128
VMEM_LIMIT = 56 * 1024 * 1024

D_MODEL = 1024
HEAD_DIM = 64
SB_WIDTH = 512
SGU_WIDTH = 512
SGU_GROUPS = 8
SGU_CHUNK = 128
SSM_GROUP = 16
SSM_STATE = 64
FFN_HIDDEN = 2816
RMS_EPS = 1e-6

TOKEN_TILE = 512
SB_BLOCK = 128
SB_HEADS = SB_WIDTH // HEAD_DIM
SB_PAIRS = SB_WIDTH // LANES
SB_CUTOFF = -104.0
S5_T = 16
S5_GROUPS_PER_TILE = LANES // SSM_GROUP
S5_TILE_STATES = S5_GROUPS_PER_TILE * SSM_STATE


def _rmsnorm(x, g):
    return x * lax.rsqrt(jnp.mean(x * x, axis=-1, keepdims=True) + RMS_EPS) * g


def _gelu(x):
    return 0.5 * x * (1.0 + jnp.tanh(math.sqrt(2.0 / math.pi) * (x + 0.044715 * (x * x * x))))


def _sigmoid(x):
    return 1.0 / (1.0 + jnp.exp(-x))


def _dot(a, b):
    return jnp.dot(a, b, preferred_element_type=F32)


def _dot_nt(a, b):
    return lax.dot_general(a, b, (((1,), (1,)), ((), ())), preferred_element_type=F32)


def _params(*sem):
    return pltpu.CompilerParams(dimension_semantics=sem, vmem_limit_bytes=VMEM_LIMIT)


def _resident(shape):
    nd = len(shape)
    return pl.BlockSpec(shape, lambda *_: (0,) * nd, pipeline_mode=pl.Buffered(1))


def _proj_kernel(x_ref, g_ref, w_ref, q_ref, k_ref, v_ref, kb_ref, vb_ref, u_ref, gv_ref):
    h = _rmsnorm(x_ref[...], g_ref[...]).astype(BF16)
    col = lambda i: _dot(h, w_ref[:, i * SB_WIDTH:(i + 1) * SB_WIDTH])
    q_ref[...] = (col(0) * (HEAD_DIM ** -0.5)).astype(BF16)
    for f32_ref, bf16_ref, z in ((k_ref, kb_ref, col(1)), (v_ref, vb_ref, col(2))):
        bf16_ref[...] = z.astype(BF16)
        f32_ref[...] = z.reshape(f32_ref.shape)
    u_ref[...] = _gelu(col(3))
    gv_ref[...] = _gelu(col(4))


def _proj(x, g, w_bf16):
    m = x.shape[0]
    tile = lambda w: pl.BlockSpec((TOKEN_TILE, w), lambda i: (i, 0))
    heads = pl.BlockSpec((TOKEN_TILE, SB_HEADS, HEAD_DIM), lambda i: (i, 0, 0))
    flat = lambda dt: jax.ShapeDtypeStruct((m, SB_WIDTH), dt)
    by_head = jax.ShapeDtypeStruct((m, SB_HEADS, HEAD_DIM), F32)
    return pl.pallas_call(
        _proj_kernel,
        grid=(m // TOKEN_TILE,),
        in_specs=[tile(D_MODEL), _resident((1, D_MODEL)), _resident(w_bf16.shape)],
        out_specs=[tile(SB_WIDTH), heads, heads] + [tile(SB_WIDTH)] * 4,
        out_shape=[flat(BF16), by_head, by_head, flat(BF16), flat(BF16), flat(F32), flat(F32)],
        compiler_params=_params("parallel"),
        name="proj",
    )(x, g, w_bf16)


def _suffix_and_total():
    j = lax.broadcasted_iota(jnp.int32, (SB_BLOCK, 2 * SB_BLOCK), 0)
    s = lax.broadcasted_iota(jnp.int32, (SB_BLOCK, 2 * SB_BLOCK), 1)
    return jnp.where(jnp.logical_or(j > s, s >= SB_BLOCK), 1.0, 0.0).astype(BF16)


def _sb_weights(z, c, sums_mat, mask):
    rows = z.shape[0]
    log_beta = jnp.minimum(z, 0.0) - jnp.log(1.0 + jnp.exp(-jnp.abs(z)))
    log_stay = log_beta - z
    if mask is not None:
        log_stay = jnp.where(mask, log_stay, 0.0)
    hi = log_stay.astype(BF16)
    lo = (log_stay - hi.astype(F32)).astype(BF16)
    sums = _dot(jnp.concatenate([hi, lo], axis=0), sums_mat)
    sums = sums[:rows] + sums[rows:]
    w = jnp.exp(log_beta + sums[:, :SB_BLOCK] + c)
    if mask is not None:
        w = jnp.where(mask, w, 0.0)
    return w.astype(BF16), c + sums[:, SB_BLOCK:]


def _sb_alive(c):
    return jnp.max(c) > SB_CUTOFF


def _sb_pair_block(q_ref, o_ref, sums_mat):
    nq = q_ref.shape[0]
    lane = lax.broadcasted_iota(jnp.int32, (nq, LANES), 1)
    q_pairs = []
    for p in range(SB_PAIRS):
        q = q_ref[:, p * LANES:(p + 1) * LANES]
        zero = jnp.zeros_like(q)
        q_pairs.append(jnp.concatenate([jnp.where(lane < HEAD_DIM, q, zero),
                                        jnp.where(lane < HEAD_DIM, zero, q)], axis=0))

    def block(kblk, vblk, c, mask, first, transposed=False):
        qk = _dot if transposed else _dot_nt
        pv_dot = _dot_nt if transposed else _dot
        z = jnp.concatenate([qk(q_pairs[p], kblk[p]) for p in range(SB_PAIRS)], axis=0)
        w, c = _sb_weights(z, c, sums_mat, mask)
        for p in range(SB_PAIRS):
            cols = slice(p * LANES, (p + 1) * LANES)
            pv = pv_dot(w[2 * p * nq:(2 * p + 2) * nq], vblk[p])
            merged = jnp.where(lane < HEAD_DIM, pv[:nq], pv[nq:])
            o_ref[:, cols] = merged if first else o_ref[:, cols] + merged
        return c

    return block


def _pair_cols(x):
    return [x[:, p * LANES:(p + 1) * LANES] for p in range(SB_PAIRS)]


def _sb_causal(nq):
    row = lax.broadcasted_iota(jnp.int32, (SB_HEADS * nq, SB_BLOCK), 0)
    col = lax.broadcasted_iota(jnp.int32, (SB_HEADS * nq, SB_BLOCK), 1)
    return col < (row & (nq - 1))


def _sb_prompt_kernel(q_ref, k_ref, v_ref, o_ref):
    qb = pl.program_id(1)
    block = _sb_pair_block(q_ref, o_ref, _suffix_and_total())
    rows = pl.ds(pl.multiple_of(qb * SB_BLOCK, SB_BLOCK), SB_BLOCK)
    c = block(_pair_cols(k_ref[rows, :]), _pair_cols(v_ref[rows, :]),
              jnp.zeros((SB_HEADS * SB_BLOCK, SB_BLOCK), F32), _sb_causal(SB_BLOCK), True)

    def cond(state):
        kb, c = state
        return jnp.logical_and(kb >= 0, _sb_alive(c))

    def body(state):
        kb, c = state
        rows = pl.ds(pl.multiple_of(kb * SB_BLOCK, SB_BLOCK), SB_BLOCK)
        return kb - 1, block(_pair_cols(k_ref[rows, :]), _pair_cols(v_ref[rows, :]), c, None, False)

    lax.while_loop(cond, body, (qb - 1, c))


def _sb_prompt(q, k, v, bsz, seq):
    nq = seq // SB_BLOCK
    qspec = pl.BlockSpec((SB_BLOCK, SB_WIDTH), lambda b, i: (b * nq + i, 0))
    kvspec = pl.BlockSpec((seq, SB_WIDTH), lambda b, i: (b, 0), pipeline_mode=pl.Buffered(1))
    return pl.pallas_call(
        _sb_prompt_kernel,
        grid=(bsz, nq),
        in_specs=[qspec, kvspec, kvspec],
        out_specs=qspec,
        out_shape=jax.ShapeDtypeStruct((bsz * seq, SB_WIDTH), F32),
        compiler_params=_params("parallel", "arbitrary"),
        name="sb_prompt",
    )(q, k, v)


def _sb_sample_kernel(q_ref, k_ref, v_ref, ck_hbm, cv_hbm, o_ref, kbuf, vbuf, sem):
    b = pl.program_id(0)
    nq = q_ref.shape[0]
    last = ck_hbm.shape[3] // SB_BLOCK - 1

    def fetch(kb, slot):
        pos = pl.ds(pl.multiple_of(kb * SB_BLOCK, SB_BLOCK), SB_BLOCK)
        return (pltpu.make_async_copy(ck_hbm.at[b, :, :, pos], kbuf.at[slot], sem.at[0, slot]),
                pltpu.make_async_copy(cv_hbm.at[b, :, :, pos], vbuf.at[slot], sem.at[1, slot]))

    def pair_rows(buf, slot):
        return [buf[slot, 2 * p:2 * p + 2].reshape(2 * HEAD_DIM, SB_BLOCK).astype(BF16) for p in range(SB_PAIRS)]

    for cp in fetch(last, last % 2):
        cp.start()

    sums_mat = _suffix_and_total()
    pad = jnp.zeros((SB_BLOCK - k_ref.shape[0], SB_WIDTH), BF16)
    new_k = jnp.concatenate([k_ref[...], pad], axis=0)
    new_v = jnp.concatenate([v_ref[...], pad], axis=0)
    block = _sb_pair_block(q_ref, o_ref, sums_mat)
    c = block(_pair_cols(new_k), _pair_cols(new_v), jnp.zeros((SB_HEADS * nq, SB_BLOCK), F32),
              _sb_causal(nq), True)

    def cond(state):
        kb, c = state
        return jnp.logical_and(kb >= 0, _sb_alive(c))

    def body(state):
        kb, c = state
        slot = kb % 2
        for cp in fetch(kb, slot):
            cp.wait()

        @pl.when(kb >= 1)
        def _():
            for cp in fetch(kb - 1, 1 - slot):
                cp.start()

        return kb - 1, block(pair_rows(kbuf, slot), pair_rows(vbuf, slot), c, None, False, transposed=True)

    kb_end, _ = lax.while_loop(cond, body, (jnp.int32(last), c))

    @pl.when(kb_end >= 0)
    def _():
        for cp in fetch(kb_end, kb_end % 2):
            cp.wait()


def _sb_sample(q, k, v, cache_k, cache_v, bsz, n):
    new = pl.BlockSpec((n, SB_WIDTH), lambda b: (b, 0))
    old = pl.BlockSpec(memory_space=pl.ANY)
    slots = pltpu.VMEM((2, SB_HEADS, HEAD_DIM, SB_BLOCK), F32)
    return pl.pallas_call(
        _sb_sample_kernel,
        grid=(bsz,),
        in_specs=[new, new, new, old, old],
        out_specs=new,
        out_shape=jax.ShapeDtypeStruct((bsz * n, SB_WIDTH), F32),
        scratch_shapes=[slots, slots, pltpu.SemaphoreType.DMA((2, 2))],
        compiler_params=_params("arbitrary"),
        name="sb_sample",
    )(q, k, v, cache_k, cache_v)


def _mix_out_kernel(x_ref, att_ref, u_ref, gv_ref, ws_ref, bs_ref, wo_ref, o_ref, sg_ref):
    chunk = ws_ref.shape[1]
    t = lax.broadcasted_iota(jnp.int32, (chunk, chunk), 0)
    s = lax.broadcasted_iota(jnp.int32, (chunk, chunk), 1)
    tri = s <= t
    ws = [jnp.where(tri, ws_ref[g], 0.0).astype(BF16) for g in range(SGU_GROUPS)]
    lane = lax.broadcasted_iota(jnp.int32, (chunk, LANES), 1)
    group_dim = SGU_WIDTH // SGU_GROUPS
    for c in range(TOKEN_TILE // chunk):
        rows = slice(c * chunk, (c + 1) * chunk)
        for p in range(SGU_WIDTH // LANES):
            cols = slice(p * LANES, (p + 1) * LANES)
            gv = gv_ref[rows, cols].astype(BF16)
            mixed = jnp.where(lane < group_dim, _dot(ws[2 * p], gv), _dot(ws[2 * p + 1], gv)) + bs_ref[:, cols]
            sg_ref[rows, cols] = (u_ref[rows, cols] * mixed).astype(BF16)
    o_ref[...] = (x_ref[...] + _dot(att_ref[...].astype(BF16), wo_ref[:SB_WIDTH, :])
                  + _dot(sg_ref[...], wo_ref[SB_WIDTH:, :]))


def _mix_out(x, att, u, gv, ws, bs_rows, wo_bf16):
    m = x.shape[0]
    tile = lambda w: pl.BlockSpec((TOKEN_TILE, w), lambda i: (i, 0))
    return pl.pallas_call(
        _mix_out_kernel,
        grid=(m // TOKEN_TILE,),
        in_specs=[tile(D_MODEL), tile(SB_WIDTH), tile(SGU_WIDTH), tile(SGU_WIDTH),
                  _resident(ws.shape), _resident(bs_rows.shape), _resident(wo_bf16.shape)],
        out_specs=tile(D_MODEL),
        out_shape=jax.ShapeDtypeStruct((m, D_MODEL), F32),
        scratch_shapes=[pltpu.VMEM((TOKEN_TILE, SGU_WIDTH), BF16)],
        compiler_params=_params("parallel"),
        name="mix_out",
    )(x, att, u, gv, ws, bs_rows, wo_bf16)


FFN_CHUNK = FFN_HIDDEN // 2


def _ffn_kernel(x_ref, g_ref, wg_ref, wu_ref, wd_ref, gn_ref, *o_refs):
    x = x_ref[...]
    h = _rmsnorm(x, g_ref[...]).astype(BF16)
    acc = x
    for c in range(FFN_HIDDEN // FFN_CHUNK):
        cols = slice(c * FFN_CHUNK, (c + 1) * FFN_CHUNK)
        gate = _dot(h, wg_ref[:, cols])
        up = _dot(h, wu_ref[:, cols])
        act = (gate * _sigmoid(gate) * up).astype(BF16)
        acc = acc + _dot(act, wd_ref[cols, :])
    if len(o_refs) == 2:
        o_refs[0][...] = acc
    o_refs[-1][...] = _rmsnorm(acc, gn_ref[...])


def _ffn(x, g, wg, wu, wd, g_next, emit_x):
    m = x.shape[0]
    tile = pl.BlockSpec((TOKEN_TILE, D_MODEL), lambda i: (i, 0))
    n_out = 2 if emit_x else 1
    return pl.pallas_call(
        _ffn_kernel,
        grid=(m // TOKEN_TILE,),
        in_specs=[tile, _resident((1, D_MODEL)), _resident(wg.shape), _resident(wu.shape),
                  _resident(wd.shape), _resident((1, D_MODEL))],
        out_specs=[tile] * n_out,
        out_shape=[jax.ShapeDtypeStruct((m, D_MODEL), F32)] * n_out,
        compiler_params=_params("parallel"),
        name="ffn",
    )(x, g, wg, wu, wd, g_next)


def _s5_tables_kernel(lr_ref, li_ref, ls_ref, bre_ref, bim_ref, cre_ref, cim_ref,
                      f_ref, et_ref, klag_ref, at_ref, f32_ref):
    half = S5_TILE_STATES
    lr, li = lr_ref[...], li_ref[...]
    step = jnp.exp(ls_ref[...])
    mag, ang = jnp.exp(lr * step), li * step
    ar, ai = mag * jnp.cos(ang), mag * jnp.sin(ang)
    den = lr * lr + li * li
    nr = ar - 1.0
    co_re = (nr * lr + ai * li) / den
    co_im = (ai * lr - nr * li) / den

    row_group = lax.broadcasted_iota(jnp.int32, (LANES, half), 0) // SSM_GROUP
    col_group = lax.broadcasted_iota(jnp.int32, (LANES, half), 1) // SSM_STATE

    def block_diag(ref):
        return jnp.where(row_group == col_group, jnp.concatenate([ref[...]] * S5_GROUPS_PER_TILE, axis=1), 0.0)

    b_re, b_im, c_re, c_im = (block_diag(r) for r in (bre_ref, bim_ref, cre_ref, cim_ref))
    g_re = co_re * b_re - co_im * b_im
    g_im = co_re * b_im + co_im * b_re
    p_re, p_im = ar, ai
    for tau in range(S5_T):
        rows = slice((S5_T - 1 - tau) * LANES, (S5_T - tau) * LANES)
        f32_ref[rows, :half] = g_re
        f32_ref[rows, half:] = g_im
        rows = slice(tau * LANES, (tau + 1) * LANES)
        et_ref[rows, :half] = (c_re * p_re - c_im * p_im).astype(BF16)
        et_ref[rows, half:] = (-c_re * p_im - c_im * p_re).astype(BF16)
        if tau < S5_T - 1:
            g_re, g_im = g_re * ar - g_im * ai, g_re * ai + g_im * ar
            p_re, p_im = p_re * ar - p_im * ai, p_re * ai + p_im * ar
    at_ref[0:1, :] = jnp.concatenate([p_re, p_re], axis=1)
    at_ref[1:2, :] = jnp.concatenate([-p_im, p_im], axis=1)

    f_all = f32_ref[...]
    f_hi = f_all.astype(BF16)
    f_ref[...] = f_hi
    f_lo = (f_all - f_hi.astype(F32)).astype(BF16)
    cc = jnp.concatenate([c_re, -c_im], axis=1)
    c_hi = cc.astype(BF16)
    c_lo = (cc - c_hi.astype(F32)).astype(BF16)
    k_all = _dot_nt(f_hi, c_hi) + _dot_nt(f_hi, c_lo) + _dot_nt(f_lo, c_hi)
    for tau in range(S5_T):
        rows = slice((S5_T - 1 - tau) * LANES, (S5_T - tau) * LANES)
        klag_ref[tau] = k_all[rows, :].astype(BF16)


def _s5_tables(lam_re, lam_im, log_step, b_re, b_im, c_re, c_im):
    groups, nstate = lam_re.shape
    ntile = groups // S5_GROUPS_PER_TILE
    half = S5_TILE_STATES
    vec = lambda a: a.reshape(ntile, 1, half)
    mat = lambda a: a.reshape(ntile, LANES, nstate)
    args = (vec(lam_re), vec(lam_im), vec(jnp.repeat(log_step, nstate)),
            mat(jnp.swapaxes(b_re, 1, 2)), mat(jnp.swapaxes(b_im, 1, 2)), mat(c_re), mat(c_im))
    per_tile = lambda shape: pl.BlockSpec((None,) + shape, lambda k: (k,) + (0,) * len(shape))
    return pl.pallas_call(
        _s5_tables_kernel,
        grid=(ntile,),
        in_specs=[per_tile((1, half))] * 3 + [per_tile((LANES, nstate))] * 4,
        out_specs=[per_tile((S5_T * LANES, 2 * half)), per_tile((S5_T * LANES, 2 * half)),
                   per_tile((S5_T, LANES, LANES)), per_tile((2, 2 * half))],
        out_shape=[jax.ShapeDtypeStruct((ntile, S5_T * LANES, 2 * half), BF16),
                   jax.ShapeDtypeStruct((ntile, S5_T * LANES, 2 * half), BF16),
                   jax.ShapeDtypeStruct((ntile, S5_T, LANES, LANES), BF16),
                   jax.ShapeDtypeStruct((ntile, 2, 2 * half), F32)],
        scratch_shapes=[pltpu.VMEM((S5_T * LANES, 2 * half), F32)],
        compiler_params=_params("parallel"),
        name="s5_tables",
    )(*args)


def _s5_kernel(h_ref, re0_ref, im0_ref, klag_ref, f_ref, et_ref, at_ref, d_ref,
               y_ref, reo_ref, imo_ref, wint_ref, xend_ref, sprev_ref, s_ref, *, nseq, seq_rows):
    first_of_tile = jnp.logical_and(pl.program_id(1) == 0, pl.program_id(2) == 0)
    nchunk = seq_rows // S5_T
    half = S5_TILE_STATES

    @pl.when(first_of_tile)
    def _():
        wint_ref[...] = jnp.zeros_like(wint_ref)
        for j in range(S5_T):
            for t in range(j, S5_T):
                wint_ref[j * LANES:(j + 1) * LANES, t * LANES:(t + 1) * LANES] = klag_ref[t - j]

    @pl.when(pl.program_id(2) == 0)
    def _():
        s_ref[:, :half] = re0_ref[...]
        s_ref[:, half:] = im0_ref[...]

    def slab(t):
        if nseq == 1:
            return [pl.ds(t, nchunk, stride=S5_T)]
        return [pl.ds(n * S5_T + t, nseq, stride=seq_rows) for n in range(nchunk)]

    u = []
    for t in range(S5_T):
        parts = [h_ref[rows, :] for rows in slab(t)]
        u.append(parts[0] if len(parts) == 1 else jnp.concatenate(parts, axis=0))
    uc = jnp.concatenate([ut.astype(BF16) for ut in u], axis=1)

    xend_ref[...] = _dot(uc, f_ref[...])

    a1 = at_ref[0:1, :]
    a2 = at_ref[1:2, :]

    def step(n, s):
        rows = pl.ds(n * nseq, nseq)
        sprev_ref[rows, :] = s
        swapped = jnp.concatenate([s[:, half:], s[:, :half]], axis=1)
        return a1 * s + a2 * swapped + xend_ref[rows, :]

    s_fin = lax.fori_loop(0, nchunk, step, s_ref[...])
    s_ref[...] = s_fin

    @pl.when(pl.program_id(2) == pl.num_programs(2) - 1)
    def _():
        reo_ref[...] = s_fin[:, :half]
        imo_ref[...] = s_fin[:, half:]

    sprev = sprev_ref[...].astype(BF16)
    d = d_ref[...]
    pair = 2 * LANES
    for cp in range(S5_T // 2):
        depth = (2 * cp + 2) * LANES
        cols = slice(cp * pair, (cp + 1) * pair)
        y2 = _dot(uc[:, :depth], wint_ref[:depth, cols]) + _dot_nt(sprev, et_ref[cols, :])
        for i in range(2):
            t = 2 * cp + i
            yt = _gelu(y2[:, i * LANES:(i + 1) * LANES] + d * u[t])
            off = 0
            for rows in slab(t):
                y_ref[rows, :] = yt[off:off + rows.size, :]
                off += rows.size


def _s5(h, re0, im0, klag, f_tab, et_tab, a_t, d_skip, nseq, seq_rows, blocks_per_seq):
    tokens = h.shape[0]
    nb = re0.shape[0]
    ntile = D_MODEL // LANES
    half = S5_TILE_STATES
    rows = nseq * seq_rows
    nrow = rows // S5_T
    hspec = pl.BlockSpec((rows, LANES), lambda k, b, j: (b * blocks_per_seq + j, k))
    sspec = pl.BlockSpec((None, nseq, half), lambda k, b, j: (b, 0, k))
    per_tile = lambda shape: pl.BlockSpec((None,) + shape, lambda k, b, j: (k,) + (0,) * len(shape))
    kern = functools.partial(_s5_kernel, nseq=nseq, seq_rows=seq_rows)
    return pl.pallas_call(
        kern,
        grid=(ntile, nb, blocks_per_seq),
        in_specs=[hspec, sspec, sspec,
                  per_tile((S5_T, LANES, LANES)), per_tile((S5_T * LANES, 2 * half)),
                  per_tile((S5_T * LANES, 2 * half)), per_tile((2, 2 * half)),
                  pl.BlockSpec((1, LANES), lambda k, b, j: (0, k))],
        out_specs=[hspec, sspec, sspec],
        out_shape=[jax.ShapeDtypeStruct((tokens, D_MODEL), F32),
                   jax.ShapeDtypeStruct(re0.shape, F32), jax.ShapeDtypeStruct(im0.shape, F32)],
        scratch_shapes=[pltpu.VMEM((S5_T * LANES, S5_T * LANES), BF16),
                        pltpu.VMEM((nrow, 2 * half), F32),
                        pltpu.VMEM((nrow, 2 * half), F32),
                        pltpu.VMEM((nseq, 2 * half), F32)],
        compiler_params=_params("arbitrary", "arbitrary", "arbitrary"),
        name="s5",
    )(h, re0, im0, klag, f_tab, et_tab, a_t, d_skip)


def _glu_kernel(x_ref, y_ref, w_ref, o_ref):
    y = y_ref[...].astype(BF16)
    ga = _dot(y, w_ref[:, :D_MODEL])
    gb = _dot(y, w_ref[:, D_MODEL:])
    o_ref[...] = x_ref[...] + ga * _sigmoid(gb)


def _glu(x, y, w_bf16):
    m = x.shape[0]
    tile = pl.BlockSpec((TOKEN_TILE, D_MODEL), lambda i: (i, 0))
    return pl.pallas_call(
        _glu_kernel,
        grid=(m // TOKEN_TILE,),
        in_specs=[tile, tile, _resident(w_bf16.shape)],
        out_specs=tile,
        out_shape=jax.ShapeDtypeStruct((m, D_MODEL), F32),
        compiler_params=_params("parallel"),
        name="glu",
    )(x, y, w_bf16)


def kernel(x_prompt, x_sample, cache_sb_k, cache_sb_v, state_ssm_re, state_ssm_im, norm_mix, norm_ffn,
           norm_final, ab_w_in, sgu_w, sgu_b, ab_w_out, ssm_lam_re, ssm_lam_im, ssm_log_step, ssm_b_re,
           ssm_b_im, ssm_c_re, ssm_c_im, ssm_d, ssm_w_glu, ffn_w_gate, ffn_w_up, ffn_w_down):
    bsz, seq, _ = x_prompt.shape
    dbsz, dseq, _ = x_sample.shape
    past = cache_sb_k.shape[2]
    heads = SB_WIDTH // HEAD_DIM
    row = lambda v: v.reshape(1, -1)

    xp = x_prompt.reshape(bsz * seq, D_MODEL)
    xs = x_sample.reshape(dbsz * dseq, D_MODEL)

    w_in = ab_w_in[0].astype(BF16)
    w_out = ab_w_out[0].astype(BF16)
    bs_rows = jnp.repeat(sgu_b[0].T, SGU_WIDTH // SGU_GROUPS, axis=1)
    g_mix0 = row(norm_mix[0])

    qp, kp, vp, kpb, vpb, up, gp = _proj(xp, g_mix0, w_in)
    qs, ks, vs, ksb, vsb, us, gs = _proj(xs, g_mix0, w_in)
    att_p = _sb_prompt(qp, kpb, vpb, bsz, seq)
    to_hdp = lambda c: jnp.transpose(c[0], (0, 2, 3, 1))
    att_s = _sb_sample(qs, ksb, vsb, to_hdp(cache_sb_k), to_hdp(cache_sb_v), dbsz, dseq)
    xp = _mix_out(xp, att_p, up, gp, sgu_w[0], bs_rows, w_out)
    xs = _mix_out(xs, att_s, us, gs, sgu_w[0][:, :dseq, :dseq], bs_rows[:dseq], w_out)

    ffn_w = [(ffn_w_gate[l].astype(BF16), ffn_w_up[l].astype(BF16), ffn_w_down[l].astype(BF16)) for l in range(2)]
    g_mix1 = row(norm_mix[1])
    xp, hp = _ffn(xp, row(norm_ffn[0]), *ffn_w[0], g_mix1, True)
    xs, hs = _ffn(xs, row(norm_ffn[0]), *ffn_w[0], g_mix1, True)

    f_tab, et_tab, klag, a_t = _s5_tables(ssm_lam_re[0], ssm_lam_im[0], ssm_log_step[0], ssm_b_re[0],
                                          ssm_b_im[0], ssm_c_re[0], ssm_c_im[0])
    d_skip = row(ssm_d[0])
    nstates = state_ssm_re.shape[2] * state_ssm_re.shape[3]
    zeros = jnp.zeros((bsz, 1, nstates), F32)
    prompt_block = 4096
    yp, rp, ip = _s5(hp, zeros, zeros, klag, f_tab, et_tab, a_t, d_skip, 1, prompt_block, seq // prompt_block)
    ys, rs, is_ = _s5(hs, state_ssm_re[0].reshape(1, dbsz, nstates), state_ssm_im[0].reshape(1, dbsz, nstates),
                      klag, f_tab, et_tab, a_t, d_skip, dbsz, dseq, 1)
    w_glu = ssm_w_glu[0].astype(BF16)
    xp = _glu(xp, yp, w_glu)
    xs = _glu(xs, ys, w_glu)
    (y_prompt,) = _ffn(xp, row(norm_ffn[1]), *ffn_w[1], row(norm_final), False)
    (y_sample,) = _ffn(xs, row(norm_ffn[1]), *ffn_w[1], row(norm_final), False)

    state_shape = state_ssm_re.shape[2:]
    return (y_prompt.reshape(bsz, seq, D_MODEL), y_sample.reshape(dbsz, dseq, D_MODEL),
            kp.reshape(1, bsz, seq, heads, HEAD_DIM), vp.reshape(1, bsz, seq, heads, HEAD_DIM),
            ks.reshape(1, dbsz, dseq, heads, HEAD_DIM), vs.reshape(1, dbsz, dseq, heads, HEAD_DIM),
            gs.reshape(1, dbsz, dseq, SGU_WIDTH),
            rp.reshape((1, bsz) + state_shape), ip.reshape((1, bsz) + state_shape),
            rs.reshape((1, dbsz) + state_shape), is_.reshape((1, dbsz) + state_shape))
```

```python
import functools
import math

import jax
import jax.numpy as jnp
from jax import lax
from jax.experimental import pallas as pl
from jax.experimental.pallas import tpu as pltpu

F32 = jnp.float32
BF16 = jnp.bfloat16

LANES = 128
VMEM_LIMIT = 56 * 1024 * 1024

D_MODEL = 1024
HEAD_DIM = 64
SB_WIDTH = 512
SGU_WIDTH = 512
SGU_GROUPS = 8
SGU_CHUNK = 128
SSM_GROUP = 16
SSM_STATE = 64
FFN_HIDDEN = 2816
RMS_EPS = 1e-6

TOKEN_TILE = 512
SB_BLOCK = 128
SB_HEADS = SB_WIDTH // HEAD_DIM
SB_PAIRS = SB_WIDTH // LANES
SB_CUTOFF = -104.0
SB_PROMPT_FUSED_BLOCKS = 3
S5_T = 16
S5_GROUPS_PER_TILE = LANES // SSM_GROUP
S5_TILE_STATES = S5_GROUPS_PER_TILE * SSM_STATE
S5_HALF_LANES = LANES // 2
S5_HALF_STATES = S5_TILE_STATES // 2
S5_HALF_TABLE = (2, S5_T * S5_HALF_LANES, 2 * S5_HALF_STATES)


def _rmsnorm(x, g):
    return x * lax.rsqrt(jnp.mean(x * x, axis=-1, keepdims=True) + RMS_EPS) * g


def _gelu(x):
    return 0.5 * x * (1.0 + jnp.tanh(math.sqrt(2.0 / math.pi) * (x + 0.044715 * (x * x * x))))


def _sigmoid(x):
    return 1.0 / (1.0 + jnp.exp(-x))


def _dot(a, b):
    return jnp.dot(a, b, preferred_element_type=F32)


def _dot_nt(a, b):
    return lax.dot_general(a, b, (((1,), (1,)), ((), ())), preferred_element_type=F32)


def _params(*sem):
    return pltpu.CompilerParams(dimension_semantics=sem, vmem_limit_bytes=VMEM_LIMIT)


def _resident(shape):
    nd = len(shape)
    return pl.BlockSpec(shape, lambda *_: (0,) * nd, pipeline_mode=pl.Buffered(1))


def _proj_kernel(x_ref, g_ref, w_ref, q_ref, k_ref, v_ref, kb_ref, vb_ref, u_ref, gv_ref):
    h = _rmsnorm(x_ref[...], g_ref[...]).astype(BF16)
    col = lambda i: _dot(h, w_ref[:, i * SB_WIDTH:(i + 1) * SB_WIDTH])
    q_ref[...] = (col(0) * (HEAD_DIM ** -0.5)).astype(BF16)
    for f32_ref, bf16_ref, z in ((k_ref, kb_ref, col(1)), (v_ref, vb_ref, col(2))):
        bf16_ref[...] = z.astype(BF16)
        f32_ref[...] = z.reshape(f32_ref.shape)
    u_ref[...] = _gelu(col(3))
    gv_ref[...] = _gelu(col(4))


def _proj(x, g, w_bf16):
    m = x.shape[0]
    tile = lambda w: pl.BlockSpec((TOKEN_TILE, w), lambda i: (i, 0))
    heads = pl.BlockSpec((TOKEN_TILE, SB_HEADS, HEAD_DIM), lambda i: (i, 0, 0))
    flat = lambda dt: jax.ShapeDtypeStruct((m, SB_WIDTH), dt)
    by_head = jax.ShapeDtypeStruct((m, SB_HEADS, HEAD_DIM), F32)
    return pl.pallas_call(
        _proj_kernel,
        grid=(m // TOKEN_TILE,),
        in_specs=[tile(D_MODEL), _resident((1, D_MODEL)), _resident(w_bf16.shape)],
        out_specs=[tile(SB_WIDTH), heads, heads] + [tile(SB_WIDTH)] * 4,
        out_shape=[flat(BF16), by_head, by_head, flat(BF16), flat(BF16), flat(F32), flat(F32)],
        compiler_params=_params("parallel"),
        name="proj",
    )(x, g, w_bf16)


def _suffix_and_total():
    j = lax.broadcasted_iota(jnp.int32, (SB_BLOCK, 2 * SB_BLOCK), 0)
    s = lax.broadcasted_iota(jnp.int32, (SB_BLOCK, 2 * SB_BLOCK), 1)
    return jnp.where(jnp.logical_or(j > s, s >= SB_BLOCK), 1.0, 0.0).astype(BF16)


def _sb_weights(z, c, sums_mat, masks):
    rows = z.shape[0]
    nb = len(masks)
    log_beta = jnp.minimum(z, 0.0) - jnp.log(1.0 + jnp.exp(-jnp.abs(z)))
    log_stay = log_beta - z
    parts = []
    for j, mask in enumerate(masks):
        stay = log_stay[:, j * SB_BLOCK:(j + 1) * SB_BLOCK]
        if mask is not None:
            stay = jnp.where(mask, stay, 0.0)
        hi = stay.astype(BF16)
        parts += [hi, (stay - hi.astype(F32)).astype(BF16)]
    sums = _dot(jnp.concatenate(parts, axis=0), sums_mat)
    ws = [None] * nb
    for j in reversed(range(nb)):
        s = sums[2 * j * rows:(2 * j + 1) * rows] + sums[(2 * j + 1) * rows:(2 * j + 2) * rows]
        w = jnp.exp(log_beta[:, j * SB_BLOCK:(j + 1) * SB_BLOCK] + s[:, :SB_BLOCK] + c)
        if masks[j] is not None:
            w = jnp.where(masks[j], w, 0.0)
        ws[j] = w.astype(BF16)
        c = c + s[:, SB_BLOCK:]
    return (ws[0] if nb == 1 else jnp.concatenate(ws, axis=1)), c


def _sb_alive(c):
    return jnp.max(c) > SB_CUTOFF


def _sb_pair_block(q_ref, o_ref, sums_mat):
    nq = q_ref.shape[0]
    lane = lax.broadcasted_iota(jnp.int32, (nq, LANES), 1)
    q_pairs = []
    for p in range(SB_PAIRS):
        q = q_ref[:, p * LANES:(p + 1) * LANES]
        zero = jnp.zeros_like(q)
        q_pairs.append(jnp.concatenate([jnp.where(lane < HEAD_DIM, q, zero),
                                        jnp.where(lane < HEAD_DIM, zero, q)], axis=0))

    def block(kblk, vblk, c, masks, first, transposed=False):
        qk = _dot if transposed else _dot_nt
        pv_dot = _dot_nt if transposed else _dot
        z = jnp.concatenate([qk(q_pairs[p], kblk[p]) for p in range(SB_PAIRS)], axis=0)
        w, c = _sb_weights(z, c, sums_mat, masks)
        for p in range(SB_PAIRS):
            cols = slice(p * LANES, (p + 1) * LANES)
            pv = pv_dot(w[2 * p * nq:(2 * p + 2) * nq], vblk[p])
            merged = jnp.where(lane < HEAD_DIM, pv[:nq], pv[nq:])
            o_ref[:, cols] = merged if first else o_ref[:, cols] + merged
        return c

    return block


def _pair_cols(x):
    return [x[:, p * LANES:(p + 1) * LANES] for p in range(SB_PAIRS)]


def _sb_causal(nq):
    row = lax.broadcasted_iota(jnp.int32, (SB_HEADS * nq, SB_BLOCK), 0)
    col = lax.broadcasted_iota(jnp.int32, (SB_HEADS * nq, SB_BLOCK), 1)
    return col < (row & (nq - 1))


def _sb_prompt_kernel(q_ref, k_ref, v_ref, o_ref):
    qb = pl.program_id(1)
    block = _sb_pair_block(q_ref, o_ref, _suffix_and_total())
    causal = _sb_causal(SB_BLOCK)
    c_init = jnp.zeros((SB_HEADS * SB_BLOCK, SB_BLOCK), F32)

    def newest(nb):
        rows = pl.ds(pl.multiple_of((qb - (nb - 1)) * SB_BLOCK, SB_BLOCK), nb * SB_BLOCK)
        return block(_pair_cols(k_ref[rows, :]), _pair_cols(v_ref[rows, :]), c_init,
                     [None] * (nb - 1) + [causal], True)

    fused = SB_PROMPT_FUSED_BLOCKS
    c = lax.cond(qb >= fused - 1, lambda: newest(fused), lambda: newest(1))

    def cond(state):
        kb, c = state
        return jnp.logical_and(kb >= 0, _sb_alive(c))

    def body(state):
        kb, c = state
        rows = pl.ds(pl.multiple_of(kb * SB_BLOCK, SB_BLOCK), SB_BLOCK)
        return kb - 1, block(_pair_cols(k_ref[rows, :]), _pair_cols(v_ref[rows, :]), c, [None], False)

    lax.while_loop(cond, body, (jnp.where(qb >= fused - 1, qb - fused, qb - 1), c))


def _sb_prompt(q, k, v, bsz, seq):
    nq = seq // SB_BLOCK
    qspec = pl.BlockSpec((SB_BLOCK, SB_WIDTH), lambda b, i: (b * nq + i, 0))
    kvspec = pl.BlockSpec((seq, SB_WIDTH), lambda b, i: (b, 0), pipeline_mode=pl.Buffered(1))
    return pl.pallas_call(
        _sb_prompt_kernel,
        grid=(bsz, nq),
        in_specs=[qspec, kvspec, kvspec],
        out_specs=qspec,
        out_shape=jax.ShapeDtypeStruct((bsz * seq, SB_WIDTH), F32),
        compiler_params=_params("parallel", "arbitrary"),
        name="sb_prompt",
    )(q, k, v)


def _sb_sample_kernel(q_ref, k_ref, v_ref, ck_hbm, cv_hbm, o_ref, kbuf, vbuf, sem):
    b = pl.program_id(0)
    nq = q_ref.shape[0]
    last = ck_hbm.shape[3] // SB_BLOCK - 1

    def fetch(kb, slot):
        pos = pl.ds(pl.multiple_of(kb * SB_BLOCK, SB_BLOCK), SB_BLOCK)
        return (pltpu.make_async_copy(ck_hbm.at[b, :, :, pos], kbuf.at[slot], sem.at[0, slot]),
                pltpu.make_async_copy(cv_hbm.at[b, :, :, pos], vbuf.at[slot], sem.at[1, slot]))

    def pair_rows(buf, slot):
        return [buf[slot, 2 * p:2 * p + 2].reshape(2 * HEAD_DIM, SB_BLOCK).astype(BF16) for p in range(SB_PAIRS)]

    for cp in fetch(last, last % 2):
        cp.start()

    sums_mat = _suffix_and_total()
    pad = jnp.zeros((SB_BLOCK - k_ref.shape[0], SB_WIDTH), BF16)
    new_k = jnp.concatenate([k_ref[...], pad], axis=0)
    new_v = jnp.concatenate([v_ref[...], pad], axis=0)
    block = _sb_pair_block(q_ref, o_ref, sums_mat)
    c = block(_pair_cols(new_k), _pair_cols(new_v), jnp.zeros((SB_HEADS * nq, SB_BLOCK), F32),
              [_sb_causal(nq)], True)

    def cond(state):
        kb, c = state
        return jnp.logical_and(kb >= 0, _sb_alive(c))

    def body(state):
        kb, c = state
        slot = kb % 2
        for cp in fetch(kb, slot):
            cp.wait()

        @pl.when(kb >= 1)
        def _():
            for cp in fetch(kb - 1, 1 - slot):
                cp.start()

        return kb - 1, block(pair_rows(kbuf, slot), pair_rows(vbuf, slot), c, [None], False, transposed=True)

    kb_end, _ = lax.while_loop(cond, body, (jnp.int32(last), c))

    @pl.when(kb_end >= 0)
    def _():
        for cp in fetch(kb_end, kb_end % 2):
            cp.wait()


def _sb_sample(q, k, v, cache_k, cache_v, bsz, n):
    new = pl.BlockSpec((n, SB_WIDTH), lambda b: (b, 0))
    old = pl.BlockSpec(memory_space=pl.ANY)
    slots = pltpu.VMEM((2, SB_HEADS, HEAD_DIM, SB_BLOCK), F32)
    return pl.pallas_call(
        _sb_sample_kernel,
        grid=(bsz,),
        in_specs=[new, new, new, old, old],
        out_specs=new,
        out_shape=jax.ShapeDtypeStruct((bsz * n, SB_WIDTH), F32),
        scratch_shapes=[slots, slots, pltpu.SemaphoreType.DMA((2, 2))],
        compiler_params=_params("arbitrary"),
        name="sb_sample",
    )(q, k, v, cache_k, cache_v)


FFN_CHUNK = FFN_HIDDEN // 2


def _ffn_tail(x, g_ref, wg_ref, wu_ref, wd_ref, gn_ref, o_refs):
    h = _rmsnorm(x, g_ref[...]).astype(BF16)
    acc = x
    for c in range(FFN_HIDDEN // FFN_CHUNK):
        cols = slice(c * FFN_CHUNK, (c + 1) * FFN_CHUNK)
        gate = _dot(h, wg_ref[:, cols])
        up = _dot(h, wu_ref[:, cols])
        act = (gate * _sigmoid(gate) * up).astype(BF16)
        acc = acc + _dot(act, wd_ref[cols, :])
    if len(o_refs) == 2:
        o_refs[0][...] = acc
    o_refs[-1][...] = _rmsnorm(acc, gn_ref[...])


def _mix_ffn_kernel(x_ref, att_ref, u_ref, gv_ref, ws_ref, bs_ref, wo_ref,
                    g_ref, wg_ref, wu_ref, wd_ref, gn_ref, xo_ref, ho_ref, sg_ref):
    chunk = ws_ref.shape[1]
    t = lax.broadcasted_iota(jnp.int32, (chunk, chunk), 0)
    s = lax.broadcasted_iota(jnp.int32, (chunk, chunk), 1)
    tri = s <= t
    ws = [jnp.where(tri, ws_ref[g], 0.0).astype(BF16) for g in range(SGU_GROUPS)]
    lane = lax.broadcasted_iota(jnp.int32, (chunk, LANES), 1)
    group_dim = SGU_WIDTH // SGU_GROUPS
    for c in range(TOKEN_TILE // chunk):
        rows = slice(c * chunk, (c + 1) * chunk)
        for p in range(SGU_WIDTH // LANES):
            cols = slice(p * LANES, (p + 1) * LANES)
            gv = gv_ref[rows, cols].astype(BF16)
            mixed = jnp.where(lane < group_dim, _dot(ws[2 * p], gv), _dot(ws[2 * p + 1], gv)) + bs_ref[:, cols]
            sg_ref[rows, cols] = (u_ref[rows, cols] * mixed).astype(BF16)
    xo_ref[...] = (x_ref[...] + _dot(att_ref[...].astype(BF16), wo_ref[:SB_WIDTH, :])
                   + _dot(sg_ref[...], wo_ref[SB_WIDTH:, :]))
    _ffn_tail(xo_ref[...], g_ref, wg_ref, wu_ref, wd_ref, gn_ref, (xo_ref, ho_ref))


def _glu_ffn_kernel(x_ref, y_ref, w_ref, g_ref, wg_ref, wu_ref, wd_ref, gn_ref, ho_ref):
    y = y_ref[...].astype(BF16)
    ga = _dot(y, w_ref[:, :D_MODEL])
    gb = _dot(y, w_ref[:, D_MODEL:])
    x = x_ref[...] + ga * _sigmoid(gb)
    _ffn_tail(x, g_ref, wg_ref, wu_ref, wd_ref, gn_ref, (ho_ref,))


def _ffn_specs(g, ffn_w, g_next):
    wg, wu, wd = ffn_w
    return ([_resident((1, D_MODEL)), _resident(wg.shape), _resident(wu.shape), _resident(wd.shape),
             _resident((1, D_MODEL))], (g, wg, wu, wd, g_next))


def _mix_ffn(x, att, u, gv, ws, bs_rows, wo_bf16, g, ffn_w, g_next):
    m = x.shape[0]
    tile = lambda w: pl.BlockSpec((TOKEN_TILE, w), lambda i: (i, 0))
    ffn_specs, ffn_args = _ffn_specs(g, ffn_w, g_next)
    return pl.pallas_call(
        _mix_ffn_kernel,
        grid=(m // TOKEN_TILE,),
        in_specs=[tile(D_MODEL), tile(SB_WIDTH), tile(SGU_WIDTH), tile(SGU_WIDTH),
                  _resident(ws.shape), _resident(bs_rows.shape), _resident(wo_bf16.shape)] + ffn_specs,
        out_specs=[tile(D_MODEL)] * 2,
        out_shape=[jax.ShapeDtypeStruct((m, D_MODEL), F32)] * 2,
        scratch_shapes=[pltpu.VMEM((TOKEN_TILE, SGU_WIDTH), BF16)],
        compiler_params=_params("parallel"),
        name="mix_ffn",
    )(x, att, u, gv, ws, bs_rows, wo_bf16, *ffn_args)


def _glu_ffn(x, y, w_glu_bf16, g, ffn_w, g_next):
    m = x.shape[0]
    tile = pl.BlockSpec((TOKEN_TILE, D_MODEL), lambda i: (i, 0))
    ffn_specs, ffn_args = _ffn_specs(g, ffn_w, g_next)
    return pl.pallas_call(
        _glu_ffn_kernel,
        grid=(m // TOKEN_TILE,),
        in_specs=[tile, tile, _resident(w_glu_bf16.shape)] + ffn_specs,
        out_specs=tile,
        out_shape=jax.ShapeDtypeStruct((m, D_MODEL), F32),
        compiler_params=_params("parallel"),
        name="glu_ffn",
    )(x, y, w_glu_bf16, *ffn_args)


def _s5_tables_kernel(lr_ref, li_ref, ls_ref, bre_ref, bim_ref, cre_ref, cim_ref,
                      f_ref, et_ref, klag_ref, at_ref, f32_ref):
    half = S5_TILE_STATES
    lr, li = lr_ref[...], li_ref[...]
    step = jnp.exp(ls_ref[...])
    mag, ang = jnp.exp(lr * step), li * step
    ar, ai = mag * jnp.cos(ang), mag * jnp.sin(ang)
    den = lr * lr + li * li
    nr = ar - 1.0
    co_re = (nr * lr + ai * li) / den
    co_im = (ai * lr - nr * li) / den

    row_group = lax.broadcasted_iota(jnp.int32, (LANES, half), 0) // SSM_GROUP
    col_group = lax.broadcasted_iota(jnp.int32, (LANES, half), 1) // SSM_STATE

    def block_diag(ref):
        return jnp.where(row_group == col_group, jnp.concatenate([ref[...]] * S5_GROUPS_PER_TILE, axis=1), 0.0)

    b_re, b_im, c_re, c_im = (block_diag(r) for r in (bre_ref, bim_ref, cre_ref, cim_ref))
    g_re = co_re * b_re - co_im * b_im
    g_im = co_re * b_im + co_im * b_re
    p_re, p_im = ar, ai
    def put_halves(ref, block, re, im):
        for hs in range(2):
            lanes = slice(hs * S5_HALF_LANES, (hs + 1) * S5_HALF_LANES)
            states = slice(hs * S5_HALF_STATES, (hs + 1) * S5_HALF_STATES)
            rows = slice(block * S5_HALF_LANES, (block + 1) * S5_HALF_LANES)
            ref[hs, rows, :S5_HALF_STATES] = re[lanes, states].astype(BF16)
            ref[hs, rows, S5_HALF_STATES:] = im[lanes, states].astype(BF16)

    for tau in range(S5_T):
        rows = slice((S5_T - 1 - tau) * LANES, (S5_T - tau) * LANES)
        f32_ref[rows, :half] = g_re
        f32_ref[rows, half:] = g_im
        put_halves(f_ref, S5_T - 1 - tau, g_re, g_im)
        put_halves(et_ref, tau, c_re * p_re - c_im * p_im, -c_re * p_im - c_im * p_re)
        if tau < S5_T - 1:
            g_re, g_im = g_re * ar - g_im * ai, g_re * ai + g_im * ar
            p_re, p_im = p_re * ar - p_im * ai, p_re * ai + p_im * ar
    at_ref[0:1, :] = jnp.concatenate([p_re, p_re], axis=1)
    at_ref[1:2, :] = jnp.concatenate([-p_im, p_im], axis=1)

    f_all = f32_ref[...]
    f_hi = f_all.astype(BF16)
    f_lo = (f_all - f_hi.astype(F32)).astype(BF16)
    cc = jnp.concatenate([c_re, -c_im], axis=1)
    c_hi = cc.astype(BF16)
    c_lo = (cc - c_hi.astype(F32)).astype(BF16)
    k_all = _dot_nt(f_hi, c_hi) + _dot_nt(f_hi, c_lo) + _dot_nt(f_lo, c_hi)
    for tau in range(S5_T):
        rows = slice((S5_T - 1 - tau) * LANES, (S5_T - tau) * LANES)
        klag_ref[tau] = k_all[rows, :]


def _s5_tables(lam_re, lam_im, log_step, b_re, b_im, c_re, c_im):
    groups, nstate = lam_re.shape
    ntile = groups // S5_GROUPS_PER_TILE
    half = S5_TILE_STATES
    vec = lambda a: a.reshape(ntile, 1, half)
    mat = lambda a: a.reshape(ntile, LANES, nstate)
    args = (vec(lam_re), vec(lam_im), vec(jnp.repeat(log_step, nstate)),
            mat(jnp.swapaxes(b_re, 1, 2)), mat(jnp.swapaxes(b_im, 1, 2)), mat(c_re), mat(c_im))
    per_tile = lambda shape: pl.BlockSpec((None,) + shape, lambda k: (k,) + (0,) * len(shape))
    return pl.pallas_call(
        _s5_tables_kernel,
        grid=(ntile,),
        in_specs=[per_tile((1, half))] * 3 + [per_tile((LANES, nstate))] * 4,
        out_specs=[per_tile(S5_HALF_TABLE), per_tile(S5_HALF_TABLE),
                   per_tile((S5_T, LANES, LANES)), per_tile((2, 2 * half))],
        out_shape=[jax.ShapeDtypeStruct((ntile,) + S5_HALF_TABLE, BF16),
                   jax.ShapeDtypeStruct((ntile,) + S5_HALF_TABLE, BF16),
                   jax.ShapeDtypeStruct((ntile, S5_T, LANES, LANES), F32),
                   jax.ShapeDtypeStruct((ntile, 2, 2 * half), F32)],
        scratch_shapes=[pltpu.VMEM((S5_T * LANES, 2 * half), F32)],
        compiler_params=_params("parallel"),
        name="s5_tables",
    )(*args)


def _s5_kernel(h_ref, re0_ref, im0_ref, klag_ref, f_ref, et_ref, at_ref, d_ref,
               y_ref, reo_ref, imo_ref, wint_ref, xend_ref, sprev_ref, s_ref, *, nseq, seq_rows):
    first_of_tile = jnp.logical_and(pl.program_id(1) == 0, pl.program_id(2) == 0)
    nchunk = seq_rows // S5_T
    half = S5_TILE_STATES

    def halves(a, b):
        lane = lax.broadcasted_iota(jnp.int32, a.shape, 1)
        first = lane < S5_HALF_LANES
        return (jnp.where(first, a, pltpu.roll(b, S5_HALF_LANES, 1)),
                jnp.where(first, pltpu.roll(a, S5_HALF_LANES, 1), b))

    @pl.when(first_of_tile)
    def _():
        wint_ref[...] = jnp.zeros_like(wint_ref)
        zero = jnp.zeros((LANES, LANES), F32)
        for j in range(S5_T):
            for m in range(j // 2, S5_T // 2):
                lag = 2 * m - j
                low, high = halves(klag_ref[lag] if lag >= 0 else zero, klag_ref[lag + 1])
                rows = slice(j * S5_HALF_LANES, (j + 1) * S5_HALF_LANES)
                wint_ref[0, rows, m * LANES:(m + 1) * LANES] = low[:S5_HALF_LANES].astype(BF16)
                wint_ref[1, rows, m * LANES:(m + 1) * LANES] = high[S5_HALF_LANES:].astype(BF16)

    @pl.when(pl.program_id(2) == 0)
    def _():
        s_ref[:, :half] = re0_ref[...]
        s_ref[:, half:] = im0_ref[...]

    def slab(t):
        if nseq == 1:
            return [pl.ds(t, nchunk, stride=S5_T)]
        return [pl.ds(n * S5_T + t, nseq, stride=seq_rows) for n in range(nchunk)]

    u = []
    for t in range(S5_T):
        parts = [h_ref[rows, :] for rows in slab(t)]
        u.append(parts[0] if len(parts) == 1 else jnp.concatenate(parts, axis=0))
    split = [halves(u[2 * m], u[2 * m + 1]) for m in range(S5_T // 2)]
    uc = [jnp.concatenate([pair[hs].astype(BF16) for pair in split], axis=1) for hs in range(2)]

    hstates = S5_HALF_STATES
    for hs in range(2):
        x = _dot(uc[hs], f_ref[hs])
        xend_ref[:, hs * hstates:(hs + 1) * hstates] = x[:, :hstates]
        xend_ref[:, half + hs * hstates:half + (hs + 1) * hstates] = x[:, hstates:]

    a1 = at_ref[0:1, :]
    a2 = at_ref[1:2, :]

    def step(n, s):
        rows = pl.ds(n * nseq, nseq)
        sprev_ref[rows, :] = s
        swapped = jnp.concatenate([s[:, half:], s[:, :half]], axis=1)
        return a1 * s + a2 * swapped + xend_ref[rows, :]

    s_fin = lax.fori_loop(0, nchunk, step, s_ref[...])
    s_ref[...] = s_fin

    @pl.when(pl.program_id(2) == pl.num_programs(2) - 1)
    def _():
        reo_ref[...] = s_fin[:, :half]
        imo_ref[...] = s_fin[:, half:]

    sprev = [jnp.concatenate([sprev_ref[:, hs * hstates:(hs + 1) * hstates],
                              sprev_ref[:, half + hs * hstates:half + (hs + 1) * hstates]], axis=1).astype(BF16)
             for hs in range(2)]
    d = d_ref[...]
    pair = 2 * LANES
    for cp in range(S5_T // 4):
        depth = (4 * cp + 4) * S5_HALF_LANES
        cols = slice(cp * pair, (cp + 1) * pair)
        y2 = [_dot(uc[hs][:, :depth], wint_ref[hs, :depth, cols]) + _dot_nt(sprev[hs], et_ref[hs, cols, :])
              for hs in range(2)]
        for i in range(2):
            m = 2 * cp + i
            tokens = halves(y2[0][:, i * LANES:(i + 1) * LANES], y2[1][:, i * LANES:(i + 1) * LANES])
            for t, y in zip((2 * m, 2 * m + 1), tokens):
                yt = _gelu(y + d * u[t])
                off = 0
                for rows in slab(t):
                    y_ref[rows, :] = yt[off:off + rows.size, :]
                    off += rows.size


def _s5(h, re0, im0, klag, f_tab, et_tab, a_t, d_skip, nseq, seq_rows, blocks_per_seq):
    tokens = h.shape[0]
    nb = re0.shape[0]
    ntile = D_MODEL // LANES
    half = S5_TILE_STATES
    rows = nseq * seq_rows
    nrow = rows // S5_T
    hspec = pl.BlockSpec((rows, LANES), lambda k, b, j: (b * blocks_per_seq + j, k))
    sspec = pl.BlockSpec((None, nseq, half), lambda k, b, j: (b, 0, k))
    per_tile = lambda shape: pl.BlockSpec((None,) + shape, lambda k, b, j: (k,) + (0,) * len(shape))
    kern = functools.partial(_s5_kernel, nseq=nseq, seq_rows=seq_rows)
    return pl.pallas_call(
        kern,
        grid=(ntile, nb, blocks_per_seq),
        in_specs=[hspec, sspec, sspec,
                  per_tile((S5_T, LANES, LANES)), per_tile(S5_HALF_TABLE), per_tile(S5_HALF_TABLE),
                  per_tile((2, 2 * half)),
                  pl.BlockSpec((1, LANES), lambda k, b, j: (0, k))],
        out_specs=[hspec, sspec, sspec],
        out_shape=[jax.ShapeDtypeStruct((tokens, D_MODEL), F32),
                   jax.ShapeDtypeStruct(re0.shape, F32), jax.ShapeDtypeStruct(im0.shape, F32)],
        scratch_shapes=[pltpu.VMEM((2, S5_T * S5_HALF_LANES, S5_T * S5_HALF_LANES), BF16),
                        pltpu.VMEM((nrow, 2 * half), F32),
                        pltpu.VMEM((nrow, 2 * half), F32),
                        pltpu.VMEM((nseq, 2 * half), F32)],
        compiler_params=_params("arbitrary", "arbitrary", "arbitrary"),
        name="s5",
    )(h, re0, im0, klag, f_tab, et_tab, a_t, d_skip)


def kernel(x_prompt, x_sample, cache_sb_k, cache_sb_v, state_ssm_re, state_ssm_im, norm_mix, norm_ffn,
           norm_final, ab_w_in, sgu_w, sgu_b, ab_w_out, ssm_lam_re, ssm_lam_im, ssm_log_step, ssm_b_re,
           ssm_b_im, ssm_c_re, ssm_c_im, ssm_d, ssm_w_glu, ffn_w_gate, ffn_w_up, ffn_w_down):
    bsz, seq, _ = x_prompt.shape
    dbsz, dseq, _ = x_sample.shape
    past = cache_sb_k.shape[2]
    heads = SB_WIDTH // HEAD_DIM
    row = lambda v: v.reshape(1, -1)

    xp = x_prompt.reshape(bsz * seq, D_MODEL)
    xs = x_sample.reshape(dbsz * dseq, D_MODEL)

    w_in = ab_w_in[0].astype(BF16)
    w_out = ab_w_out[0].astype(BF16)
    bs_rows = jnp.repeat(sgu_b[0].T, SGU_WIDTH // SGU_GROUPS, axis=1)
    g_mix0 = row(norm_mix[0])

    qp, kp, vp, kpb, vpb, up, gp = _proj(xp, g_mix0, w_in)
    qs, ks, vs, ksb, vsb, us, gs = _proj(xs, g_mix0, w_in)
    att_p = _sb_prompt(qp, kpb, vpb, bsz, seq)
    to_hdp = lambda c: jnp.transpose(c[0], (0, 2, 3, 1))
    att_s = _sb_sample(qs, ksb, vsb, to_hdp(cache_sb_k), to_hdp(cache_sb_v), dbsz, dseq)
    ffn_w = [(ffn_w_gate[l].astype(BF16), ffn_w_up[l].astype(BF16), ffn_w_down[l].astype(BF16)) for l in range(2)]
    g_mix1 = row(norm_mix[1])
    xp, hp = _mix_ffn(xp, att_p, up, gp, sgu_w[0], bs_rows, w_out, row(norm_ffn[0]), ffn_w[0], g_mix1)
    xs, hs = _mix_ffn(xs, att_s, us, gs, sgu_w[0][:, :dseq, :dseq], bs_rows[:dseq], w_out,
                      row(norm_ffn[0]), ffn_w[0], g_mix1)

    f_tab, et_tab, klag, a_t = _s5_tables(ssm_lam_re[0], ssm_lam_im[0], ssm_log_step[0], ssm_b_re[0],
                                          ssm_b_im[0], ssm_c_re[0], ssm_c_im[0])
    d_skip = row(ssm_d[0])
    nstates = state_ssm_re.shape[2] * state_ssm_re.shape[3]
    zeros = jnp.zeros((bsz, 1, nstates), F32)
    prompt_block = 4096
    yp, rp, ip = _s5(hp, zeros, zeros, klag, f_tab, et_tab, a_t, d_skip, 1, prompt_block, seq // prompt_block)
    ys, rs, is_ = _s5(hs, state_ssm_re[0].reshape(1, dbsz, nstates), state_ssm_im[0].reshape(1, dbsz, nstates),
                      klag, f_tab, et_tab, a_t, d_skip, dbsz, dseq, 1)
    w_glu = ssm_w_glu[0].astype(BF16)
    y_prompt = _glu_ffn(xp, yp, w_glu, row(norm_ffn[1]), ffn_w[1], row(norm_final))
    y_sample = _glu_ffn(xs, ys, w_glu, row(norm_ffn[1]), ffn_w[1], row(norm_final))

    state_shape = state_ssm_re.shape[2:]
    return (y_prompt.reshape(bsz, seq, D_MODEL), y_sample.reshape(dbsz, dseq, D_MODEL),
            kp.reshape(1, bsz, seq, heads, HEAD_DIM), vp.reshape(1, bsz, seq, heads, HEAD_DIM),
            ks.reshape(1, dbsz, dseq, heads, HEAD_DIM), vs.reshape(1, dbsz, dseq, heads, HEAD_DIM),
            gs.reshape(1, dbsz, dseq, SGU_WIDTH),
            rp.reshape((1, bsz) + state_shape), ip.reshape((1, bsz) + state_shape),
            rs.reshape((1, dbsz) + state_shape), is_.reshape((1, dbsz) + state_shape))
```

```python
import functools
import math

import jax
import jax.numpy as jnp
from jax import lax
from jax.experimental import pallas as pl
from jax.experimental.pallas import tpu as pltpu

F32 = jnp.float32
BF16 = jnp.bfloat16

LANES = 128
VMEM_LIMIT = 56 * 1024 * 1024

D_MODEL = 1024
HEAD_DIM = 64
SB_WIDTH = 512
SGU_WIDTH = 512
SGU_GROUPS = 8
SGU_CHUNK = 128
SSM_GROUP = 16
SSM_STATE = 64
FFN_HIDDEN = 2816
RMS_EPS = 1e-6

TOKEN_TILE = 512
SB_BLOCK = 128
SB_HEADS = SB_WIDTH // HEAD_DIM
SB_PAIRS = SB_WIDTH // LANES
SB_CUTOFF = -104.0
SB_PROMPT_FUSED_BLOCKS = 3
S5_T = 16
S5_SCAN_UNROLL = 8
S5_GROUPS_PER_TILE = LANES // SSM_GROUP
S5_TILE_STATES = S5_GROUPS_PER_TILE * SSM_STATE
S5_HALF_LANES = LANES // 2
S5_HALF_STATES = S5_TILE_STATES // 2
S5_HALF_TABLE = (2, S5_T * S5_HALF_LANES, 2 * S5_HALF_STATES)


def _rmsnorm(x, g):
    return x * lax.rsqrt(jnp.mean(x * x, axis=-1, keepdims=True) + RMS_EPS) * g


def _gelu(x):
    return 0.5 * x * (1.0 + jnp.tanh(math.sqrt(2.0 / math.pi) * (x + 0.044715 * (x * x * x))))


def _sigmoid(x):
    return 1.0 / (1.0 + jnp.exp(-x))


def _dot(a, b):
    return jnp.dot(a, b, preferred_element_type=F32)


def _dot_nt(a, b):
    return lax.dot_general(a, b, (((1,), (1,)), ((), ())), preferred_element_type=F32)


def _params(*sem):
    return pltpu.CompilerParams(dimension_semantics=sem, vmem_limit_bytes=VMEM_LIMIT)


def _resident(shape):
    nd = len(shape)
    return pl.BlockSpec(shape, lambda *_: (0,) * nd, pipeline_mode=pl.Buffered(1))


def _proj_kernel(x_ref, g_ref, w_ref, q_ref, k_ref, v_ref, kb_ref, vb_ref, u_ref, gv_ref):
    h = _rmsnorm(x_ref[...], g_ref[...]).astype(BF16)
    col = lambda i: _dot(h, w_ref[:, i * SB_WIDTH:(i + 1) * SB_WIDTH])
    q_ref[...] = (col(0) * (HEAD_DIM ** -0.5)).astype(BF16)
    for f32_ref, bf16_ref, z in ((k_ref, kb_ref, col(1)), (v_ref, vb_ref, col(2))):
        bf16_ref[...] = z.astype(BF16)
        f32_ref[...] = z.reshape(f32_ref.shape)
    u_ref[...] = _gelu(col(3))
    gv_ref[...] = _gelu(col(4))


def _proj(x, g, w_bf16):
    m = x.shape[0]
    tile = lambda w: pl.BlockSpec((TOKEN_TILE, w), lambda i: (i, 0))
    heads = pl.BlockSpec((TOKEN_TILE, SB_HEADS, HEAD_DIM), lambda i: (i, 0, 0))
    flat = lambda dt: jax.ShapeDtypeStruct((m, SB_WIDTH), dt)
    by_head = jax.ShapeDtypeStruct((m, SB_HEADS, HEAD_DIM), F32)
    return pl.pallas_call(
        _proj_kernel,
        grid=(m // TOKEN_TILE,),
        in_specs=[tile(D_MODEL), _resident((1, D_MODEL)), _resident(w_bf16.shape)],
        out_specs=[tile(SB_WIDTH), heads, heads] + [tile(SB_WIDTH)] * 4,
        out_shape=[flat(BF16), by_head, by_head, flat(BF16), flat(BF16), flat(F32), flat(F32)],
        compiler_params=_params("parallel"),
        name="proj",
    )(x, g, w_bf16)


def _suffix_and_total():
    j = lax.broadcasted_iota(jnp.int32, (SB_BLOCK, 2 * SB_BLOCK), 0)
    s = lax.broadcasted_iota(jnp.int32, (SB_BLOCK, 2 * SB_BLOCK), 1)
    return jnp.where(jnp.logical_or(j > s, s >= SB_BLOCK), 1.0, 0.0).astype(BF16)


def _sb_weights(z, c, sums_mat, masks):
    rows = z.shape[0]
    nb = len(masks)
    log_beta = jnp.minimum(z, 0.0) - jnp.log(1.0 + jnp.exp(-jnp.abs(z)))
    log_stay = log_beta - z
    parts = []
    for j, mask in enumerate(masks):
        stay = log_stay[:, j * SB_BLOCK:(j + 1) * SB_BLOCK]
        if mask is not None:
            stay = jnp.where(mask, stay, 0.0)
        parts.append(stay.astype(BF16))
    sums = _dot(parts[0] if nb == 1 else jnp.concatenate(parts, axis=0), sums_mat)
    ws = [None] * nb
    for j in reversed(range(nb)):
        s = sums[j * rows:(j + 1) * rows]
        w = jnp.exp(log_beta[:, j * SB_BLOCK:(j + 1) * SB_BLOCK] + s[:, :SB_BLOCK] + c)
        if masks[j] is not None:
            w = jnp.where(masks[j], w, 0.0)
        ws[j] = w.astype(BF16)
        c = c + s[:, SB_BLOCK:]
    return (ws[0] if nb == 1 else jnp.concatenate(ws, axis=1)), c


def _sb_alive(c):
    return jnp.max(c) > SB_CUTOFF


def _sb_pair_block(q_ref, o_ref, sums_mat):
    nq = q_ref.shape[0]
    lane = lax.broadcasted_iota(jnp.int32, (nq, LANES), 1)
    q_pairs = []
    for p in range(SB_PAIRS):
        q = q_ref[:, p * LANES:(p + 1) * LANES]
        zero = jnp.zeros_like(q)
        q_pairs.append(jnp.concatenate([jnp.where(lane < HEAD_DIM, q, zero),
                                        jnp.where(lane < HEAD_DIM, zero, q)], axis=0))

    def block(kblk, vblk, c, masks, first, transposed=False):
        qk = _dot if transposed else _dot_nt
        pv_dot = _dot_nt if transposed else _dot
        z = jnp.concatenate([qk(q_pairs[p], kblk[p]) for p in range(SB_PAIRS)], axis=0)
        w, c = _sb_weights(z, c, sums_mat, masks)
        for p in range(SB_PAIRS):
            cols = slice(p * LANES, (p + 1) * LANES)
            pv = pv_dot(w[2 * p * nq:(2 * p + 2) * nq], vblk[p])
            merged = jnp.where(lane < HEAD_DIM, pv[:nq], pv[nq:])
            o_ref[:, cols] = merged if first else o_ref[:, cols] + merged
        return c

    return block


def _pair_cols(x):
    return [x[:, p * LANES:(p + 1) * LANES] for p in range(SB_PAIRS)]


def _sb_causal(nq):
    row = lax.broadcasted_iota(jnp.int32, (SB_HEADS * nq, SB_BLOCK), 0)
    col = lax.broadcasted_iota(jnp.int32, (SB_HEADS * nq, SB_BLOCK), 1)
    return col < (row & (nq - 1))


def _sb_prompt_kernel(q_ref, k_ref, v_ref, o_ref):
    qb = pl.program_id(1)
    block = _sb_pair_block(q_ref, o_ref, _suffix_and_total())
    causal = _sb_causal(SB_BLOCK)
    c_init = jnp.zeros((SB_HEADS * SB_BLOCK, SB_BLOCK), F32)

    def newest(nb):
        rows = pl.ds(pl.multiple_of((qb - (nb - 1)) * SB_BLOCK, SB_BLOCK), nb * SB_BLOCK)
        return block(_pair_cols(k_ref[rows, :]), _pair_cols(v_ref[rows, :]), c_init,
                     [None] * (nb - 1) + [causal], True)

    fused = SB_PROMPT_FUSED_BLOCKS
    c = lax.cond(qb >= fused - 1, lambda: newest(fused), lambda: newest(1))

    def cond(state):
        kb, c = state
        return jnp.logical_and(kb >= 0, _sb_alive(c))

    def body(state):
        kb, c = state
        rows = pl.ds(pl.multiple_of(kb * SB_BLOCK, SB_BLOCK), SB_BLOCK)
        return kb - 1, block(_pair_cols(k_ref[rows, :]), _pair_cols(v_ref[rows, :]), c, [None], False)

    lax.while_loop(cond, body, (jnp.where(qb >= fused - 1, qb - fused, qb - 1), c))


def _sb_prompt(q, k, v, bsz, seq):
    nq = seq // SB_BLOCK
    qspec = pl.BlockSpec((SB_BLOCK, SB_WIDTH), lambda b, i: (b * nq + i, 0))
    kvspec = pl.BlockSpec((seq, SB_WIDTH), lambda b, i: (b, 0), pipeline_mode=pl.Buffered(1))
    return pl.pallas_call(
        _sb_prompt_kernel,
        grid=(bsz, nq),
        in_specs=[qspec, kvspec, kvspec],
        out_specs=qspec,
        out_shape=jax.ShapeDtypeStruct((bsz * seq, SB_WIDTH), F32),
        compiler_params=_params("parallel", "arbitrary"),
        name="sb_prompt",
    )(q, k, v)


def _sb_sample_kernel(q_ref, k_ref, v_ref, ck_hbm, cv_hbm, o_ref, kbuf, vbuf, sem):
    b = pl.program_id(0)
    nq = q_ref.shape[0]
    last = ck_hbm.shape[3] // SB_BLOCK - 1

    def fetch(kb, slot):
        pos = pl.ds(pl.multiple_of(kb * SB_BLOCK, SB_BLOCK), SB_BLOCK)
        return (pltpu.make_async_copy(ck_hbm.at[b, :, :, pos], kbuf.at[slot], sem.at[0, slot]),
                pltpu.make_async_copy(cv_hbm.at[b, :, :, pos], vbuf.at[slot], sem.at[1, slot]))

    def pair_rows(buf, slot):
        return [buf[slot, 2 * p:2 * p + 2].reshape(2 * HEAD_DIM, SB_BLOCK).astype(BF16) for p in range(SB_PAIRS)]

    for cp in fetch(last, last % 2):
        cp.start()

    sums_mat = _suffix_and_total()
    pad = jnp.zeros((SB_BLOCK - k_ref.shape[0], SB_WIDTH), BF16)
    new_k = jnp.concatenate([k_ref[...], pad], axis=0)
    new_v = jnp.concatenate([v_ref[...], pad], axis=0)
    block = _sb_pair_block(q_ref, o_ref, sums_mat)
    c = block(_pair_cols(new_k), _pair_cols(new_v), jnp.zeros((SB_HEADS * nq, SB_BLOCK), F32),
              [_sb_causal(nq)], True)

    def cond(state):
        kb, c = state
        return jnp.logical_and(kb >= 0, _sb_alive(c))

    def body(state):
        kb, c = state
        slot = kb % 2
        for cp in fetch(kb, slot):
            cp.wait()

        @pl.when(kb >= 1)
        def _():
            for cp in fetch(kb - 1, 1 - slot):
                cp.start()

        return kb - 1, block(pair_rows(kbuf, slot), pair_rows(vbuf, slot), c, [None], False, transposed=True)

    kb_end, _ = lax.while_loop(cond, body, (jnp.int32(last), c))

    @pl.when(kb_end >= 0)
    def _():
        for cp in fetch(kb_end, kb_end % 2):
            cp.wait()


def _sb_sample(q, k, v, cache_k, cache_v, bsz, n):
    new = pl.BlockSpec((n, SB_WIDTH), lambda b: (b, 0))
    old = pl.BlockSpec(memory_space=pl.ANY)
    slots = pltpu.VMEM((2, SB_HEADS, HEAD_DIM, SB_BLOCK), F32)
    return pl.pallas_call(
        _sb_sample_kernel,
        grid=(bsz,),
        in_specs=[new, new, new, old, old],
        out_specs=new,
        out_shape=jax.ShapeDtypeStruct((bsz * n, SB_WIDTH), F32),
        scratch_shapes=[slots, slots, pltpu.SemaphoreType.DMA((2, 2))],
        compiler_params=_params("arbitrary"),
        name="sb_sample",
    )(q, k, v, cache_k, cache_v)


FFN_CHUNK = FFN_HIDDEN // 2


def _ffn_tail(x, g_ref, wg_ref, wu_ref, wd_ref, gn_ref, o_refs):
    h = _rmsnorm(x, g_ref[...]).astype(BF16)
    acc = x
    for c in range(FFN_HIDDEN // FFN_CHUNK):
        cols = slice(c * FFN_CHUNK, (c + 1) * FFN_CHUNK)
        gate = _dot(h, wg_ref[:, cols])
        up = _dot(h, wu_ref[:, cols])
        act = (gate * _sigmoid(gate) * up).astype(BF16)
        acc = acc + _dot(act, wd_ref[cols, :])
    if len(o_refs) == 2:
        o_refs[0][...] = acc
    o_refs[-1][...] = _rmsnorm(acc, gn_ref[...])


def _mix_ffn_kernel(x_ref, att_ref, u_ref, gv_ref, ws_ref, bs_ref, wo_ref,
                    g_ref, wg_ref, wu_ref, wd_ref, gn_ref, xo_ref, ho_ref, sg_ref):
    chunk = ws_ref.shape[1]
    t = lax.broadcasted_iota(jnp.int32, (chunk, chunk), 0)
    s = lax.broadcasted_iota(jnp.int32, (chunk, chunk), 1)
    tri = s <= t
    ws = [jnp.where(tri, ws_ref[g], 0.0).astype(BF16) for g in range(SGU_GROUPS)]
    lane = lax.broadcasted_iota(jnp.int32, (chunk, LANES), 1)
    group_dim = SGU_WIDTH // SGU_GROUPS
    for c in range(TOKEN_TILE // chunk):
        rows = slice(c * chunk, (c + 1) * chunk)
        for p in range(SGU_WIDTH // LANES):
            cols = slice(p * LANES, (p + 1) * LANES)
            gv = gv_ref[rows, cols].astype(BF16)
            mixed = jnp.where(lane < group_dim, _dot(ws[2 * p], gv), _dot(ws[2 * p + 1], gv)) + bs_ref[:, cols]
            sg_ref[rows, cols] = (u_ref[rows, cols] * mixed).astype(BF16)
    xo_ref[...] = (x_ref[...] + _dot(att_ref[...].astype(BF16), wo_ref[:SB_WIDTH, :])
                   + _dot(sg_ref[...], wo_ref[SB_WIDTH:, :]))
    _ffn_tail(xo_ref[...], g_ref, wg_ref, wu_ref, wd_ref, gn_ref, (xo_ref, ho_ref))


def _glu_ffn_kernel(x_ref, y_ref, w_ref, g_ref, wg_ref, wu_ref, wd_ref, gn_ref, ho_ref):
    y = y_ref[...].astype(BF16)
    ga = _dot(y, w_ref[:, :D_MODEL])
    gb = _dot(y, w_ref[:, D_MODEL:])
    x = x_ref[...] + ga * _sigmoid(gb)
    _ffn_tail(x, g_ref, wg_ref, wu_ref, wd_ref, gn_ref, (ho_ref,))


def _ffn_specs(g, ffn_w, g_next):
    wg, wu, wd = ffn_w
    return ([_resident((1, D_MODEL)), _resident(wg.shape), _resident(wu.shape), _resident(wd.shape),
             _resident((1, D_MODEL))], (g, wg, wu, wd, g_next))


def _mix_ffn(x, att, u, gv, ws, bs_rows, wo_bf16, g, ffn_w, g_next):
    m = x.shape[0]
    tile = lambda w: pl.BlockSpec((TOKEN_TILE, w), lambda i: (i, 0))
    ffn_specs, ffn_args = _ffn_specs(g, ffn_w, g_next)
    return pl.pallas_call(
        _mix_ffn_kernel,
        grid=(m // TOKEN_TILE,),
        in_specs=[tile(D_MODEL), tile(SB_WIDTH), tile(SGU_WIDTH), tile(SGU_WIDTH),
                  _resident(ws.shape), _resident(bs_rows.shape), _resident(wo_bf16.shape)] + ffn_specs,
        out_specs=[tile(D_MODEL)] * 2,
        out_shape=[jax.ShapeDtypeStruct((m, D_MODEL), F32)] * 2,
        scratch_shapes=[pltpu.VMEM((TOKEN_TILE, SGU_WIDTH), BF16)],
        compiler_params=_params("parallel"),
        name="mix_ffn",
    )(x, att, u, gv, ws, bs_rows, wo_bf16, *ffn_args)


def _glu_ffn(x, y, w_glu_bf16, g, ffn_w, g_next):
    m = x.shape[0]
    tile = pl.BlockSpec((TOKEN_TILE, D_MODEL), lambda i: (i, 0))
    ffn_specs, ffn_args = _ffn_specs(g, ffn_w, g_next)
    return pl.pallas_call(
        _glu_ffn_kernel,
        grid=(m // TOKEN_TILE,),
        in_specs=[tile, tile, _resident(w_glu_bf16.shape)] + ffn_specs,
        out_specs=tile,
        out_shape=jax.ShapeDtypeStruct((m, D_MODEL), F32),
        compiler_params=_params("parallel"),
        name="glu_ffn",
    )(x, y, w_glu_bf16, *ffn_args)


def _s5_tables_kernel(lr_ref, li_ref, ls_ref, bre_ref, bim_ref, cre_ref, cim_ref,
                      f_ref, et_ref, klag_ref, at_ref, f32_ref):
    half = S5_TILE_STATES
    lr, li = lr_ref[...], li_ref[...]
    step = jnp.exp(ls_ref[...])
    mag, ang = jnp.exp(lr * step), li * step
    ar, ai = mag * jnp.cos(ang), mag * jnp.sin(ang)
    den = lr * lr + li * li
    nr = ar - 1.0
    co_re = (nr * lr + ai * li) / den
    co_im = (ai * lr - nr * li) / den

    row_group = lax.broadcasted_iota(jnp.int32, (LANES, half), 0) // SSM_GROUP
    col_group = lax.broadcasted_iota(jnp.int32, (LANES, half), 1) // SSM_STATE

    def block_diag(ref):
        return jnp.where(row_group == col_group, jnp.concatenate([ref[...]] * S5_GROUPS_PER_TILE, axis=1), 0.0)

    b_re, b_im, c_re, c_im = (block_diag(r) for r in (bre_ref, bim_ref, cre_ref, cim_ref))
    g_re = co_re * b_re - co_im * b_im
    g_im = co_re * b_im + co_im * b_re
    p_re, p_im = ar, ai
    def put_halves(ref, block, re, im):
        for hs in range(2):
            lanes = slice(hs * S5_HALF_LANES, (hs + 1) * S5_HALF_LANES)
            states = slice(hs * S5_HALF_STATES, (hs + 1) * S5_HALF_STATES)
            rows = slice(block * S5_HALF_LANES, (block + 1) * S5_HALF_LANES)
            ref[hs, rows, :S5_HALF_STATES] = re[lanes, states].astype(BF16)
            ref[hs, rows, S5_HALF_STATES:] = im[lanes, states].astype(BF16)

    for tau in range(S5_T):
        rows = slice((S5_T - 1 - tau) * LANES, (S5_T - tau) * LANES)
        f32_ref[rows, :half] = g_re
        f32_ref[rows, half:] = g_im
        put_halves(f_ref, S5_T - 1 - tau, g_re, g_im)
        put_halves(et_ref, tau, c_re * p_re - c_im * p_im, -c_re * p_im - c_im * p_re)
        if tau < S5_T - 1:
            g_re, g_im = g_re * ar - g_im * ai, g_re * ai + g_im * ar
            p_re, p_im = p_re * ar - p_im * ai, p_re * ai + p_im * ar
    at_ref[0:1, :] = jnp.concatenate([p_re, p_re], axis=1)
    at_ref[1:2, :] = jnp.concatenate([-p_im, p_im], axis=1)

    f_all = f32_ref[...]
    f_hi = f_all.astype(BF16)
    f_lo = (f_all - f_hi.astype(F32)).astype(BF16)
    cc = jnp.concatenate([c_re, -c_im], axis=1)
    c_hi = cc.astype(BF16)
    c_lo = (cc - c_hi.astype(F32)).astype(BF16)
    k_all = _dot_nt(f_hi, c_hi) + _dot_nt(f_hi, c_lo) + _dot_nt(f_lo, c_hi)
    for tau in range(S5_T):
        rows = slice((S5_T - 1 - tau) * LANES, (S5_T - tau) * LANES)
        klag_ref[tau] = k_all[rows, :]


def _s5_tables(lam_re, lam_im, log_step, b_re, b_im, c_re, c_im):
    groups, nstate = lam_re.shape
    ntile = groups // S5_GROUPS_PER_TILE
    half = S5_TILE_STATES
    vec = lambda a: a.reshape(ntile, 1, half)
    mat = lambda a: a.reshape(ntile, LANES, nstate)
    args = (vec(lam_re), vec(lam_im), vec(jnp.repeat(log_step, nstate)),
            mat(jnp.swapaxes(b_re, 1, 2)), mat(jnp.swapaxes(b_im, 1, 2)), mat(c_re), mat(c_im))
    per_tile = lambda shape: pl.BlockSpec((None,) + shape, lambda k: (k,) + (0,) * len(shape))
    return pl.pallas_call(
        _s5_tables_kernel,
        grid=(ntile,),
        in_specs=[per_tile((1, half))] * 3 + [per_tile((LANES, nstate))] * 4,
        out_specs=[per_tile(S5_HALF_TABLE), per_tile(S5_HALF_TABLE),
                   per_tile((S5_T, LANES, LANES)), per_tile((2, 2 * half))],
        out_shape=[jax.ShapeDtypeStruct((ntile,) + S5_HALF_TABLE, BF16),
                   jax.ShapeDtypeStruct((ntile,) + S5_HALF_TABLE, BF16),
                   jax.ShapeDtypeStruct((ntile, S5_T, LANES, LANES), F32),
                   jax.ShapeDtypeStruct((ntile, 2, 2 * half), F32)],
        scratch_shapes=[pltpu.VMEM((S5_T * LANES, 2 * half), F32)],
        compiler_params=_params("parallel"),
        name="s5_tables",
    )(*args)


def _s5_kernel(h_ref, re0_ref, im0_ref, klag_ref, f_ref, et_ref, at_ref, d_ref,
               y_ref, reo_ref, imo_ref, wint_ref, xend_ref, sprev_ref, s_ref, *, nseq, seq_rows):
    first_of_tile = jnp.logical_and(pl.program_id(1) == 0, pl.program_id(2) == 0)
    nchunk = seq_rows // S5_T
    half = S5_TILE_STATES

    def halves(a, b):
        lane = lax.broadcasted_iota(jnp.int32, a.shape, 1)
        first = lane < S5_HALF_LANES
        return (jnp.where(first, a, pltpu.roll(b, S5_HALF_LANES, 1)),
                jnp.where(first, pltpu.roll(a, S5_HALF_LANES, 1), b))

    @pl.when(first_of_tile)
    def _():
        wint_ref[...] = jnp.zeros_like(wint_ref)
        zero = jnp.zeros((LANES, LANES), F32)
        for j in range(S5_T):
            for m in range(j // 2, S5_T // 2):
                lag = 2 * m - j
                low, high = halves(klag_ref[lag] if lag >= 0 else zero, klag_ref[lag + 1])
                rows = slice(j * S5_HALF_LANES, (j + 1) * S5_HALF_LANES)
                wint_ref[0, rows, m * LANES:(m + 1) * LANES] = low[:S5_HALF_LANES].astype(BF16)
                wint_ref[1, rows, m * LANES:(m + 1) * LANES] = high[S5_HALF_LANES:].astype(BF16)

    @pl.when(pl.program_id(2) == 0)
    def _():
        s_ref[:, :half] = re0_ref[...]
        s_ref[:, half:] = im0_ref[...]

    def slab(t):
        if nseq == 1:
            return [pl.ds(t, nchunk, stride=S5_T)]
        return [pl.ds(n * S5_T + t, nseq, stride=seq_rows) for n in range(nchunk)]

    u = []
    for t in range(S5_T):
        parts = [h_ref[rows, :] for rows in slab(t)]
        u.append(parts[0] if len(parts) == 1 else jnp.concatenate(parts, axis=0))
    split = [halves(u[2 * m], u[2 * m + 1]) for m in range(S5_T // 2)]
    uc = [jnp.concatenate([pair[hs].astype(BF16) for pair in split], axis=1) for hs in range(2)]

    hstates = S5_HALF_STATES
    for hs in range(2):
        x = _dot(uc[hs], f_ref[hs])
        xend_ref[:, hs * hstates:(hs + 1) * hstates] = x[:, :hstates]
        xend_ref[:, half + hs * hstates:half + (hs + 1) * hstates] = x[:, hstates:]

    a1 = at_ref[0:1, :]
    a2 = at_ref[1:2, :]

    def step(n, s):
        rows = pl.ds(n * nseq, nseq)
        sprev_ref[rows, :] = s
        swapped = jnp.concatenate([s[:, half:], s[:, :half]], axis=1)
        return a1 * s + a2 * swapped + xend_ref[rows, :]

    s_fin = lax.fori_loop(0, nchunk, step, s_ref[...], unroll=min(S5_SCAN_UNROLL, nchunk))
    s_ref[...] = s_fin

    @pl.when(pl.program_id(2) == pl.num_programs(2) - 1)
    def _():
        reo_ref[...] = s_fin[:, :half]
        imo_ref[...] = s_fin[:, half:]

    sprev = [jnp.concatenate([sprev_ref[:, hs * hstates:(hs + 1) * hstates],
                              sprev_ref[:, half + hs * hstates:half + (hs + 1) * hstates]], axis=1).astype(BF16)
             for hs in range(2)]
    d = d_ref[...]
    pair = 2 * LANES
    for cp in range(S5_T // 4):
        depth = (4 * cp + 4) * S5_HALF_LANES
        cols = slice(cp * pair, (cp + 1) * pair)
        y2 = [_dot(uc[hs][:, :depth], wint_ref[hs, :depth, cols]) + _dot_nt(sprev[hs], et_ref[hs, cols, :])
              for hs in range(2)]
        for i in range(2):
            m = 2 * cp + i
            tokens = halves(y2[0][:, i * LANES:(i + 1) * LANES], y2[1][:, i * LANES:(i + 1) * LANES])
            for t, y in zip((2 * m, 2 * m + 1), tokens):
                yt = _gelu(y + d * u[t])
                off = 0
                for rows in slab(t):
                    y_ref[rows, :] = yt[off:off + rows.size, :]
                    off += rows.size


def _s5(h, re0, im0, klag, f_tab, et_tab, a_t, d_skip, nseq, seq_rows, blocks_per_seq):
    tokens = h.shape[0]
    nb = re0.shape[0]
    ntile = D_MODEL // LANES
    half = S5_TILE_STATES
    rows = nseq * seq_rows
    nrow = rows // S5_T
    hspec = pl.BlockSpec((rows, LANES), lambda k, b, j: (b * blocks_per_seq + j, k))
    sspec = pl.BlockSpec((None, nseq, half), lambda k, b, j: (b, 0, k))
    per_tile = lambda shape: pl.BlockSpec((None,) + shape, lambda k, b, j: (k,) + (0,) * len(shape))
    kern = functools.partial(_s5_kernel, nseq=nseq, seq_rows=seq_rows)
    return pl.pallas_call(
        kern,
        grid=(ntile, nb, blocks_per_seq),
        in_specs=[hspec, sspec, sspec,
                  per_tile((S5_T, LANES, LANES)), per_tile(S5_HALF_TABLE), per_tile(S5_HALF_TABLE),
                  per_tile((2, 2 * half)),
                  pl.BlockSpec((1, LANES), lambda k, b, j: (0, k))],
        out_specs=[hspec, sspec, sspec],
        out_shape=[jax.ShapeDtypeStruct((tokens, D_MODEL), F32),
                   jax.ShapeDtypeStruct(re0.shape, F32), jax.ShapeDtypeStruct(im0.shape, F32)],
        scratch_shapes=[pltpu.VMEM((2, S5_T * S5_HALF_LANES, S5_T * S5_HALF_LANES), BF16),
                        pltpu.VMEM((nrow, 2 * half), F32),
                        pltpu.VMEM((nrow, 2 * half), F32),
                        pltpu.VMEM((nseq, 2 * half), F32)],
        compiler_params=_params("arbitrary", "arbitrary", "arbitrary"),
        name="s5",
    )(h, re0, im0, klag, f_tab, et_tab, a_t, d_skip)


def kernel(x_prompt, x_sample, cache_sb_k, cache_sb_v, state_ssm_re, state_ssm_im, norm_mix, norm_ffn,
           norm_final, ab_w_in, sgu_w, sgu_b, ab_w_out, ssm_lam_re, ssm_lam_im, ssm_log_step, ssm_b_re,
           ssm_b_im, ssm_c_re, ssm_c_im, ssm_d, ssm_w_glu, ffn_w_gate, ffn_w_up, ffn_w_down):
    bsz, seq, _ = x_prompt.shape
    dbsz, dseq, _ = x_sample.shape
    past = cache_sb_k.shape[2]
    heads = SB_WIDTH // HEAD_DIM
    row = lambda v: v.reshape(1, -1)

    xp = x_prompt.reshape(bsz * seq, D_MODEL)
    xs = x_sample.reshape(dbsz * dseq, D_MODEL)

    w_in = ab_w_in[0].astype(BF16)
    w_out = ab_w_out[0].astype(BF16)
    bs_rows = jnp.repeat(sgu_b[0].T, SGU_WIDTH // SGU_GROUPS, axis=1)
    g_mix0 = row(norm_mix[0])

    qp, kp, vp, kpb, vpb, up, gp = _proj(xp, g_mix0, w_in)
    qs, ks, vs, ksb, vsb, us, gs = _proj(xs, g_mix0, w_in)
    att_p = _sb_prompt(qp, kpb, vpb, bsz, seq)
    to_hdp = lambda c: jnp.transpose(c[0], (0, 2, 3, 1))
    att_s = _sb_sample(qs, ksb, vsb, to_hdp(cache_sb_k), to_hdp(cache_sb_v), dbsz, dseq)
    ffn_w = [(ffn_w_gate[l].astype(BF16), ffn_w_up[l].astype(BF16), ffn_w_down[l].astype(BF16)) for l in range(2)]
    g_mix1 = row(norm_mix[1])
    xp, hp = _mix_ffn(xp, att_p, up, gp, sgu_w[0], bs_rows, w_out, row(norm_ffn[0]), ffn_w[0], g_mix1)
    xs, hs = _mix_ffn(xs, att_s, us, gs, sgu_w[0][:, :dseq, :dseq], bs_rows[:dseq], w_out,
                      row(norm_ffn[0]), ffn_w[0], g_mix1)

    f_tab, et_tab, klag, a_t = _s5_tables(ssm_lam_re[0], ssm_lam_im[0], ssm_log_step[0], ssm_b_re[0],
                                          ssm_b_im[0], ssm_c_re[0], ssm_c_im[0])
    d_skip = row(ssm_d[0])
    nstates = state_ssm_re.shape[2] * state_ssm_re.shape[3]
    zeros = jnp.zeros((bsz, 1, nstates), F32)
    prompt_block = 4096
    yp, rp, ip = _s5(hp, zeros, zeros, klag, f_tab, et_tab, a_t, d_skip, 1, prompt_block, seq // prompt_block)
    ys, rs, is_ = _s5(hs, state_ssm_re[0].reshape(1, dbsz, nstates), state_ssm_im[0].reshape(1, dbsz, nstates),
                      klag, f_tab, et_tab, a_t, d_skip, dbsz, dseq, 1)
    w_glu = ssm_w_glu[0].astype(BF16)
    y_prompt = _glu_ffn(xp, yp, w_glu, row(norm_ffn[1]), ffn_w[1], row(norm_final))
    y_sample = _glu_ffn(xs, ys, w_glu, row(norm_ffn[1]), ffn_w[1], row(norm_final))

    state_shape = state_ssm_re.shape[2:]
    return (y_prompt.reshape(bsz, seq, D_MODEL), y_sample.reshape(dbsz, dseq, D_MODEL),
            kp.reshape(1, bsz, seq, heads, HEAD_DIM), vp.reshape(1, bsz, seq, heads, HEAD_DIM),
            ks.reshape(1, dbsz, dseq, heads, HEAD_DIM), vs.reshape(1, dbsz, dseq, heads, HEAD_DIM),
            gs.reshape(1, dbsz, dseq, SGU_WIDTH),
            rp.reshape((1, bsz) + state_shape), ip.reshape((1, bsz) + state_shape),
            rs.reshape((1, dbsz) + state_shape), is_.reshape((1, dbsz) + state_shape))
```

```python
import functools
import math

import jax
import jax.numpy as jnp
from jax import lax
from jax.experimental import pallas as pl
from jax.experimental.pallas import tpu as pltpu

F32 = jnp.float32
BF16 = jnp.bfloat16

LANES = 128
VMEM_LIMIT = 56 * 1024 * 1024

D_MODEL = 1024
HEAD_DIM = 64
SB_WIDTH = 512
SGU_WIDTH = 512
SGU_GROUPS = 8
SGU_CHUNK = 128
SSM_GROUP = 16
SSM_STATE = 64
FFN_HIDDEN = 2816
RMS_EPS = 1e-6

TOKEN_TILE = 512
SB_BLOCK = 128
SB_HEADS = SB_WIDTH // HEAD_DIM
SB_PAIRS = SB_WIDTH // LANES
SB_CUTOFF = -104.0
SB_PROMPT_FUSED_BLOCKS = 3
S5_T = 16
S5_SCAN_UNROLL = 8
S5_SEGMENTS = 8
S5_GROUPS_PER_TILE = LANES // SSM_GROUP
S5_TILE_STATES = S5_GROUPS_PER_TILE * SSM_STATE
S5_HALF_LANES = LANES // 2
S5_HALF_STATES = S5_TILE_STATES // 2
S5_HALF_TABLE = (2, S5_T * S5_HALF_LANES, 2 * S5_HALF_STATES)


def _rmsnorm(x, g):
    return x * lax.rsqrt(jnp.mean(x * x, axis=-1, keepdims=True) + RMS_EPS) * g


def _gelu(x):
    return 0.5 * x * (1.0 + jnp.tanh(math.sqrt(2.0 / math.pi) * (x + 0.044715 * (x * x * x))))


def _sigmoid(x):
    return 1.0 / (1.0 + jnp.exp(-x))


def _dot(a, b):
    return jnp.dot(a, b, preferred_element_type=F32)


def _dot_nt(a, b):
    return lax.dot_general(a, b, (((1,), (1,)), ((), ())), preferred_element_type=F32)


def _params(*sem):
    return pltpu.CompilerParams(dimension_semantics=sem, vmem_limit_bytes=VMEM_LIMIT)


def _resident(shape):
    nd = len(shape)
    return pl.BlockSpec(shape, lambda *_: (0,) * nd, pipeline_mode=pl.Buffered(1))


def _proj_kernel(x_ref, g_ref, w_ref, q_ref, k_ref, v_ref, kb_ref, vb_ref, u_ref, gv_ref):
    h = _rmsnorm(x_ref[...], g_ref[...]).astype(BF16)
    col = lambda i: _dot(h, w_ref[:, i * SB_WIDTH:(i + 1) * SB_WIDTH])
    q_ref[...] = (col(0) * (HEAD_DIM ** -0.5)).astype(BF16)
    for f32_ref, bf16_ref, z in ((k_ref, kb_ref, col(1)), (v_ref, vb_ref, col(2))):
        bf16_ref[...] = z.astype(BF16)
        f32_ref[...] = z.reshape(f32_ref.shape)
    u_ref[...] = _gelu(col(3))
    gv_ref[...] = _gelu(col(4))


def _proj(x, g, w_bf16):
    m = x.shape[0]
    tile = lambda w: pl.BlockSpec((TOKEN_TILE, w), lambda i: (i, 0))
    heads = pl.BlockSpec((TOKEN_TILE, SB_HEADS, HEAD_DIM), lambda i: (i, 0, 0))
    flat = lambda dt: jax.ShapeDtypeStruct((m, SB_WIDTH), dt)
    by_head = jax.ShapeDtypeStruct((m, SB_HEADS, HEAD_DIM), F32)
    return pl.pallas_call(
        _proj_kernel,
        grid=(m // TOKEN_TILE,),
        in_specs=[tile(D_MODEL), _resident((1, D_MODEL)), _resident(w_bf16.shape)],
        out_specs=[tile(SB_WIDTH), heads, heads] + [tile(SB_WIDTH)] * 4,
        out_shape=[flat(BF16), by_head, by_head, flat(BF16), flat(BF16), flat(F32), flat(F32)],
        compiler_params=_params("parallel"),
        name="proj",
    )(x, g, w_bf16)


def _suffix_and_total():
    j = lax.broadcasted_iota(jnp.int32, (SB_BLOCK, 2 * SB_BLOCK), 0)
    s = lax.broadcasted_iota(jnp.int32, (SB_BLOCK, 2 * SB_BLOCK), 1)
    return jnp.where(jnp.logical_or(j > s, s >= SB_BLOCK), 1.0, 0.0).astype(BF16)


def _sb_weights(z, c, sums_mat, masks):
    rows = z.shape[0]
    nb = len(masks)
    log_beta = jnp.minimum(z, 0.0) - jnp.log(1.0 + jnp.exp(-jnp.abs(z)))
    log_stay = log_beta - z
    parts = []
    for j, mask in enumerate(masks):
        stay = log_stay[:, j * SB_BLOCK:(j + 1) * SB_BLOCK]
        if mask is not None:
            stay = jnp.where(mask, stay, 0.0)
        parts.append(stay.astype(BF16))
    sums = _dot(parts[0] if nb == 1 else jnp.concatenate(parts, axis=0), sums_mat)
    ws = [None] * nb
    for j in reversed(range(nb)):
        s = sums[j * rows:(j + 1) * rows]
        w = jnp.exp(log_beta[:, j * SB_BLOCK:(j + 1) * SB_BLOCK] + s[:, :SB_BLOCK] + c)
        if masks[j] is not None:
            w = jnp.where(masks[j], w, 0.0)
        ws[j] = w.astype(BF16)
        c = c + s[:, SB_BLOCK:]
    return (ws[0] if nb == 1 else jnp.concatenate(ws, axis=1)), c


def _sb_alive(c):
    return jnp.max(c) > SB_CUTOFF


def _sb_pair_block(q_ref, o_ref, sums_mat):
    nq = q_ref.shape[0]
    lane = lax.broadcasted_iota(jnp.int32, (nq, LANES), 1)
    q_pairs = []
    for p in range(SB_PAIRS):
        q = q_ref[:, p * LANES:(p + 1) * LANES]
        zero = jnp.zeros_like(q)
        q_pairs.append(jnp.concatenate([jnp.where(lane < HEAD_DIM, q, zero),
                                        jnp.where(lane < HEAD_DIM, zero, q)], axis=0))

    def block(kblk, vblk, c, masks, first, transposed=False):
        qk = _dot if transposed else _dot_nt
        pv_dot = _dot_nt if transposed else _dot
        z = jnp.concatenate([qk(q_pairs[p], kblk[p]) for p in range(SB_PAIRS)], axis=0)
        w, c = _sb_weights(z, c, sums_mat, masks)
        for p in range(SB_PAIRS):
            cols = slice(p * LANES, (p + 1) * LANES)
            pv = pv_dot(w[2 * p * nq:(2 * p + 2) * nq], vblk[p])
            merged = jnp.where(lane < HEAD_DIM, pv[:nq], pv[nq:])
            o_ref[:, cols] = merged if first else o_ref[:, cols] + merged
        return c

    return block


def _pair_cols(x):
    return [x[:, p * LANES:(p + 1) * LANES] for p in range(SB_PAIRS)]


def _sb_causal(nq):
    row = lax.broadcasted_iota(jnp.int32, (SB_HEADS * nq, SB_BLOCK), 0)
    col = lax.broadcasted_iota(jnp.int32, (SB_HEADS * nq, SB_BLOCK), 1)
    return col < (row & (nq - 1))


def _sb_prompt_kernel(q_ref, k_ref, v_ref, o_ref):
    qb = pl.program_id(1)
    block = _sb_pair_block(q_ref, o_ref, _suffix_and_total())
    causal = _sb_causal(SB_BLOCK)
    c_init = jnp.zeros((SB_HEADS * SB_BLOCK, SB_BLOCK), F32)

    def newest(nb):
        rows = pl.ds(pl.multiple_of((qb - (nb - 1)) * SB_BLOCK, SB_BLOCK), nb * SB_BLOCK)
        return block(_pair_cols(k_ref[rows, :]), _pair_cols(v_ref[rows, :]), c_init,
                     [None] * (nb - 1) + [causal], True)

    fused = SB_PROMPT_FUSED_BLOCKS
    c = lax.cond(qb >= fused - 1, lambda: newest(fused), lambda: newest(1))

    def cond(state):
        kb, c = state
        return jnp.logical_and(kb >= 0, _sb_alive(c))

    def body(state):
        kb, c = state
        rows = pl.ds(pl.multiple_of(kb * SB_BLOCK, SB_BLOCK), SB_BLOCK)
        return kb - 1, block(_pair_cols(k_ref[rows, :]), _pair_cols(v_ref[rows, :]), c, [None], False)

    lax.while_loop(cond, body, (jnp.where(qb >= fused - 1, qb - fused, qb - 1), c))


def _sb_prompt(q, k, v, bsz, seq):
    nq = seq // SB_BLOCK
    qspec = pl.BlockSpec((SB_BLOCK, SB_WIDTH), lambda b, i: (b * nq + i, 0))
    kvspec = pl.BlockSpec((seq, SB_WIDTH), lambda b, i: (b, 0), pipeline_mode=pl.Buffered(1))
    return pl.pallas_call(
        _sb_prompt_kernel,
        grid=(bsz, nq),
        in_specs=[qspec, kvspec, kvspec],
        out_specs=qspec,
        out_shape=jax.ShapeDtypeStruct((bsz * seq, SB_WIDTH), F32),
        compiler_params=_params("parallel", "arbitrary"),
        name="sb_prompt",
    )(q, k, v)


def _sb_sample_kernel(q_ref, k_ref, v_ref, ck_hbm, cv_hbm, o_ref, kbuf, vbuf, sem):
    b = pl.program_id(0)
    nq = q_ref.shape[0]
    last = ck_hbm.shape[3] // SB_BLOCK - 1

    def fetch(kb, slot):
        pos = pl.ds(pl.multiple_of(kb * SB_BLOCK, SB_BLOCK), SB_BLOCK)
        return (pltpu.make_async_copy(ck_hbm.at[b, :, :, pos], kbuf.at[slot], sem.at[0, slot]),
                pltpu.make_async_copy(cv_hbm.at[b, :, :, pos], vbuf.at[slot], sem.at[1, slot]))

    def pair_rows(buf, slot):
        return [buf[slot, 2 * p:2 * p + 2].reshape(2 * HEAD_DIM, SB_BLOCK).astype(BF16) for p in range(SB_PAIRS)]

    for cp in fetch(last, last % 2):
        cp.start()

    sums_mat = _suffix_and_total()
    pad = jnp.zeros((SB_BLOCK - k_ref.shape[0], SB_WIDTH), BF16)
    new_k = jnp.concatenate([k_ref[...], pad], axis=0)
    new_v = jnp.concatenate([v_ref[...], pad], axis=0)
    block = _sb_pair_block(q_ref, o_ref, sums_mat)
    c = block(_pair_cols(new_k), _pair_cols(new_v), jnp.zeros((SB_HEADS * nq, SB_BLOCK), F32),
              [_sb_causal(nq)], True)

    def cond(state):
        kb, c = state
        return jnp.logical_and(kb >= 0, _sb_alive(c))

    def body(state):
        kb, c = state
        slot = kb % 2
        for cp in fetch(kb, slot):
            cp.wait()

        @pl.when(kb >= 1)
        def _():
            for cp in fetch(kb - 1, 1 - slot):
                cp.start()

        return kb - 1, block(pair_rows(kbuf, slot), pair_rows(vbuf, slot), c, [None], False, transposed=True)

    kb_end, _ = lax.while_loop(cond, body, (jnp.int32(last), c))

    @pl.when(kb_end >= 0)
    def _():
        for cp in fetch(kb_end, kb_end % 2):
            cp.wait()


def _sb_sample(q, k, v, cache_k, cache_v, bsz, n):
    new = pl.BlockSpec((n, SB_WIDTH), lambda b: (b, 0))
    old = pl.BlockSpec(memory_space=pl.ANY)
    slots = pltpu.VMEM((2, SB_HEADS, HEAD_DIM, SB_BLOCK), F32)
    return pl.pallas_call(
        _sb_sample_kernel,
        grid=(bsz,),
        in_specs=[new, new, new, old, old],
        out_specs=new,
        out_shape=jax.ShapeDtypeStruct((bsz * n, SB_WIDTH), F32),
        scratch_shapes=[slots, slots, pltpu.SemaphoreType.DMA((2, 2))],
        compiler_params=_params("arbitrary"),
        name="sb_sample",
    )(q, k, v, cache_k, cache_v)


FFN_CHUNK = FFN_HIDDEN // 2


def _ffn_tail(x, g_ref, wg_ref, wu_ref, wd_ref, gn_ref, o_refs):
    h = _rmsnorm(x, g_ref[...]).astype(BF16)
    acc = x
    for c in range(FFN_HIDDEN // FFN_CHUNK):
        cols = slice(c * FFN_CHUNK, (c + 1) * FFN_CHUNK)
        gate = _dot(h, wg_ref[:, cols])
        up = _dot(h, wu_ref[:, cols])
        act = (gate * _sigmoid(gate) * up).astype(BF16)
        acc = acc + _dot(act, wd_ref[cols, :])
    if len(o_refs) == 2:
        o_refs[0][...] = acc
    o_refs[-1][...] = _rmsnorm(acc, gn_ref[...])


def _mix_ffn_kernel(x_ref, att_ref, u_ref, gv_ref, ws_ref, bs_ref, wo_ref,
                    g_ref, wg_ref, wu_ref, wd_ref, gn_ref, xo_ref, ho_ref, sg_ref):
    chunk = ws_ref.shape[1]
    t = lax.broadcasted_iota(jnp.int32, (chunk, chunk), 0)
    s = lax.broadcasted_iota(jnp.int32, (chunk, chunk), 1)
    tri = s <= t
    ws = [jnp.where(tri, ws_ref[g], 0.0).astype(BF16) for g in range(SGU_GROUPS)]
    lane = lax.broadcasted_iota(jnp.int32, (chunk, LANES), 1)
    group_dim = SGU_WIDTH // SGU_GROUPS
    for c in range(TOKEN_TILE // chunk):
        rows = slice(c * chunk, (c + 1) * chunk)
        for p in range(SGU_WIDTH // LANES):
            cols = slice(p * LANES, (p + 1) * LANES)
            gv = gv_ref[rows, cols].astype(BF16)
            mixed = jnp.where(lane < group_dim, _dot(ws[2 * p], gv), _dot(ws[2 * p + 1], gv)) + bs_ref[:, cols]
            sg_ref[rows, cols] = (u_ref[rows, cols] * mixed).astype(BF16)
    xo_ref[...] = (x_ref[...] + _dot(att_ref[...].astype(BF16), wo_ref[:SB_WIDTH, :])
                   + _dot(sg_ref[...], wo_ref[SB_WIDTH:, :]))
    _ffn_tail(xo_ref[...], g_ref, wg_ref, wu_ref, wd_ref, gn_ref, (xo_ref, ho_ref))


def _glu_ffn_kernel(x_ref, y_ref, w_ref, g_ref, wg_ref, wu_ref, wd_ref, gn_ref, ho_ref):
    y = y_ref[...].astype(BF16)
    ga = _dot(y, w_ref[:, :D_MODEL])
    gb = _dot(y, w_ref[:, D_MODEL:])
    x = x_ref[...] + ga * _sigmoid(gb)
    _ffn_tail(x, g_ref, wg_ref, wu_ref, wd_ref, gn_ref, (ho_ref,))


def _ffn_specs(g, ffn_w, g_next):
    wg, wu, wd = ffn_w
    return ([_resident((1, D_MODEL)), _resident(wg.shape), _resident(wu.shape), _resident(wd.shape),
             _resident((1, D_MODEL))], (g, wg, wu, wd, g_next))


def _mix_ffn(x, att, u, gv, ws, bs_rows, wo_bf16, g, ffn_w, g_next):
    m = x.shape[0]
    tile = lambda w: pl.BlockSpec((TOKEN_TILE, w), lambda i: (i, 0))
    ffn_specs, ffn_args = _ffn_specs(g, ffn_w, g_next)
    return pl.pallas_call(
        _mix_ffn_kernel,
        grid=(m // TOKEN_TILE,),
        in_specs=[tile(D_MODEL), tile(SB_WIDTH), tile(SGU_WIDTH), tile(SGU_WIDTH),
                  _resident(ws.shape), _resident(bs_rows.shape), _resident(wo_bf16.shape)] + ffn_specs,
        out_specs=[tile(D_MODEL)] * 2,
        out_shape=[jax.ShapeDtypeStruct((m, D_MODEL), F32)] * 2,
        scratch_shapes=[pltpu.VMEM((TOKEN_TILE, SGU_WIDTH), BF16)],
        compiler_params=_params("parallel"),
        name="mix_ffn",
    )(x, att, u, gv, ws, bs_rows, wo_bf16, *ffn_args)


def _glu_ffn(x, y, w_glu_bf16, g, ffn_w, g_next):
    m = x.shape[0]
    tile = pl.BlockSpec((TOKEN_TILE, D_MODEL), lambda i: (i, 0))
    ffn_specs, ffn_args = _ffn_specs(g, ffn_w, g_next)
    return pl.pallas_call(
        _glu_ffn_kernel,
        grid=(m // TOKEN_TILE,),
        in_specs=[tile, tile, _resident(w_glu_bf16.shape)] + ffn_specs,
        out_specs=tile,
        out_shape=jax.ShapeDtypeStruct((m, D_MODEL), F32),
        compiler_params=_params("parallel"),
        name="glu_ffn",
    )(x, y, w_glu_bf16, *ffn_args)


def _s5_tables_kernel(lr_ref, li_ref, ls_ref, bre_ref, bim_ref, cre_ref, cim_ref,
                      f_ref, et_ref, klag_ref, at_ref, f32_ref):
    half = S5_TILE_STATES
    lr, li = lr_ref[...], li_ref[...]
    step = jnp.exp(ls_ref[...])
    mag, ang = jnp.exp(lr * step), li * step
    ar, ai = mag * jnp.cos(ang), mag * jnp.sin(ang)
    den = lr * lr + li * li
    nr = ar - 1.0
    co_re = (nr * lr + ai * li) / den
    co_im = (ai * lr - nr * li) / den

    row_group = lax.broadcasted_iota(jnp.int32, (LANES, half), 0) // SSM_GROUP
    col_group = lax.broadcasted_iota(jnp.int32, (LANES, half), 1) // SSM_STATE

    def block_diag(ref):
        return jnp.where(row_group == col_group, jnp.concatenate([ref[...]] * S5_GROUPS_PER_TILE, axis=1), 0.0)

    b_re, b_im, c_re, c_im = (block_diag(r) for r in (bre_ref, bim_ref, cre_ref, cim_ref))
    g_re = co_re * b_re - co_im * b_im
    g_im = co_re * b_im + co_im * b_re
    p_re, p_im = ar, ai
    def put_halves(ref, block, re, im):
        for hs in range(2):
            lanes = slice(hs * S5_HALF_LANES, (hs + 1) * S5_HALF_LANES)
            states = slice(hs * S5_HALF_STATES, (hs + 1) * S5_HALF_STATES)
            rows = slice(block * S5_HALF_LANES, (block + 1) * S5_HALF_LANES)
            ref[hs, rows, :S5_HALF_STATES] = re[lanes, states].astype(BF16)
            ref[hs, rows, S5_HALF_STATES:] = im[lanes, states].astype(BF16)

    for tau in range(S5_T):
        rows = slice((S5_T - 1 - tau) * LANES, (S5_T - tau) * LANES)
        f32_ref[rows, :half] = g_re
        f32_ref[rows, half:] = g_im
        put_halves(f_ref, S5_T - 1 - tau, g_re, g_im)
        put_halves(et_ref, tau, c_re * p_re - c_im * p_im, -c_re * p_im - c_im * p_re)
        if tau < S5_T - 1:
            g_re, g_im = g_re * ar - g_im * ai, g_re * ai + g_im * ar
            p_re, p_im = p_re * ar - p_im * ai, p_re * ai + p_im * ar
    at_ref[0:1, :] = jnp.concatenate([p_re, p_re], axis=1)
    at_ref[1:2, :] = jnp.concatenate([-p_im, p_im], axis=1)

    f_all = f32_ref[...]
    f_hi = f_all.astype(BF16)
    f_lo = (f_all - f_hi.astype(F32)).astype(BF16)
    cc = jnp.concatenate([c_re, -c_im], axis=1)
    c_hi = cc.astype(BF16)
    c_lo = (cc - c_hi.astype(F32)).astype(BF16)
    k_all = _dot_nt(f_hi, c_hi) + _dot_nt(f_hi, c_lo) + _dot_nt(f_lo, c_hi)
    for tau in range(S5_T):
        rows = slice((S5_T - 1 - tau) * LANES, (S5_T - tau) * LANES)
        klag_ref[tau] = k_all[rows, :]


def _s5_tables(lam_re, lam_im, log_step, b_re, b_im, c_re, c_im):
    groups, nstate = lam_re.shape
    ntile = groups // S5_GROUPS_PER_TILE
    half = S5_TILE_STATES
    vec = lambda a: a.reshape(ntile, 1, half)
    mat = lambda a: a.reshape(ntile, LANES, nstate)
    args = (vec(lam_re), vec(lam_im), vec(jnp.repeat(log_step, nstate)),
            mat(jnp.swapaxes(b_re, 1, 2)), mat(jnp.swapaxes(b_im, 1, 2)), mat(c_re), mat(c_im))
    per_tile = lambda shape: pl.BlockSpec((None,) + shape, lambda k: (k,) + (0,) * len(shape))
    return pl.pallas_call(
        _s5_tables_kernel,
        grid=(ntile,),
        in_specs=[per_tile((1, half))] * 3 + [per_tile((LANES, nstate))] * 4,
        out_specs=[per_tile(S5_HALF_TABLE), per_tile(S5_HALF_TABLE),
                   per_tile((S5_T, LANES, LANES)), per_tile((2, 2 * half))],
        out_shape=[jax.ShapeDtypeStruct((ntile,) + S5_HALF_TABLE, BF16),
                   jax.ShapeDtypeStruct((ntile,) + S5_HALF_TABLE, BF16),
                   jax.ShapeDtypeStruct((ntile, S5_T, LANES, LANES), F32),
                   jax.ShapeDtypeStruct((ntile, 2, 2 * half), F32)],
        scratch_shapes=[pltpu.VMEM((S5_T * LANES, 2 * half), F32)],
        compiler_params=_params("parallel"),
        name="s5_tables",
    )(*args)


def _s5_kernel(h_ref, re0_ref, im0_ref, klag_ref, f_ref, et_ref, at_ref, d_ref,
               y_ref, reo_ref, imo_ref, wint_ref, xend_ref, sprev_ref, s_ref, pow1_ref, pow2_ref,
               *, nseq, seq_rows):
    first_of_tile = jnp.logical_and(pl.program_id(1) == 0, pl.program_id(2) == 0)
    nchunk = seq_rows // S5_T
    half = S5_TILE_STATES

    def halves(a, b):
        lane = lax.broadcasted_iota(jnp.int32, a.shape, 1)
        first = lane < S5_HALF_LANES
        return (jnp.where(first, a, pltpu.roll(b, S5_HALF_LANES, 1)),
                jnp.where(first, pltpu.roll(a, S5_HALF_LANES, 1), b))

    @pl.when(first_of_tile)
    def _():
        wint_ref[...] = jnp.zeros_like(wint_ref)
        zero = jnp.zeros((LANES, LANES), F32)
        for j in range(S5_T):
            for m in range(j // 2, S5_T // 2):
                lag = 2 * m - j
                low, high = halves(klag_ref[lag] if lag >= 0 else zero, klag_ref[lag + 1])
                rows = slice(j * S5_HALF_LANES, (j + 1) * S5_HALF_LANES)
                wint_ref[0, rows, m * LANES:(m + 1) * LANES] = low[:S5_HALF_LANES].astype(BF16)
                wint_ref[1, rows, m * LANES:(m + 1) * LANES] = high[S5_HALF_LANES:].astype(BF16)

    @pl.when(pl.program_id(2) == 0)
    def _():
        s_ref[:, :half] = re0_ref[...]
        s_ref[:, half:] = im0_ref[...]

    def slab(t):
        if nseq == 1:
            return [pl.ds(t, nchunk, stride=S5_T)]
        return [pl.ds(n * S5_T + t, nseq, stride=seq_rows) for n in range(nchunk)]

    u = []
    for t in range(S5_T):
        parts = [h_ref[rows, :] for rows in slab(t)]
        u.append(parts[0] if len(parts) == 1 else jnp.concatenate(parts, axis=0))
    split = [halves(u[2 * m], u[2 * m + 1]) for m in range(S5_T // 2)]
    uc = [jnp.concatenate([pair[hs].astype(BF16) for pair in split], axis=1) for hs in range(2)]

    nslab = 2 * half // LANES

    def load_rows(ref, rows):
        return jnp.concatenate([ref[q, rows, :] for q in range(nslab)], axis=1)

    def store_rows(ref, rows, val, add=False):
        for q in range(nslab):
            piece = val[:, q * LANES:(q + 1) * LANES]
            ref[q, rows, :] = ref[q, rows, :] + piece if add else piece

    hstates = S5_HALF_STATES
    hslabs = hstates // LANES
    for hs in range(2):
        x = _dot(uc[hs], f_ref[hs])
        for part in range(2):
            for q in range(hslabs):
                col = part * hstates + q * LANES
                xend_ref[part * (nslab // 2) + hs * hslabs + q] = x[:, col:col + LANES]

    a1 = at_ref[0:1, :]
    a2 = at_ref[1:2, :]
    swap = lambda s: jnp.concatenate([s[:, half:], s[:, :half]], axis=1)
    cmul = lambda p1, p2, s: p1 * s + p2 * swap(s)

    if nseq > 1:
        def step(n, s):
            rows = pl.ds(n * nseq, nseq)
            store_rows(sprev_ref, rows, s)
            return cmul(a1, a2, s) + load_rows(xend_ref, rows)

        s_fin = lax.fori_loop(0, nchunk, step, s_ref[...], unroll=min(S5_SCAN_UNROLL, nchunk))
    else:
        seg_len = nchunk // S5_SEGMENTS

        @pl.when(first_of_tile)
        def _():
            z = jnp.concatenate([jnp.ones((1, half), F32), jnp.zeros((1, half), F32)], axis=1)
            for i in range(seg_len + 1):
                pow1_ref[i:i + 1, :] = jnp.concatenate([z[:, :half], z[:, :half]], axis=1)
                pow2_ref[i:i + 1, :] = jnp.concatenate([-z[:, half:], z[:, half:]], axis=1)
                z = cmul(a1, a2, z)

        def local_step(i, s):
            rows = pl.ds(i, S5_SEGMENTS, stride=seg_len)
            store_rows(sprev_ref, rows, s)
            return cmul(a1, a2, s) + load_rows(xend_ref, rows)

        local_end = lax.fori_loop(0, seg_len, local_step, jnp.zeros((S5_SEGMENTS, 2 * half), F32),
                                  unroll=S5_SCAN_UNROLL)
        seg1 = pow1_ref[seg_len:seg_len + 1, :]
        seg2 = pow2_ref[seg_len:seg_len + 1, :]
        start = s_ref[...]
        for seg in range(S5_SEGMENTS):
            rows = slice(seg * seg_len, (seg + 1) * seg_len)
            store_rows(sprev_ref, rows, pow1_ref[:seg_len, :] * start + pow2_ref[:seg_len, :] * swap(start),
                       add=True)
            start = cmul(seg1, seg2, start) + local_end[seg:seg + 1, :]
        s_fin = start
    s_ref[...] = s_fin

    @pl.when(pl.program_id(2) == pl.num_programs(2) - 1)
    def _():
        reo_ref[...] = s_fin[:, :half]
        imo_ref[...] = s_fin[:, half:]

    sprev = [jnp.concatenate([sprev_ref[part * (nslab // 2) + hs * hslabs + q]
                              for part in range(2) for q in range(hslabs)], axis=1).astype(BF16)
             for hs in range(2)]
    d = d_ref[...]
    pair = 2 * LANES
    for cp in range(S5_T // 4):
        depth = (4 * cp + 4) * S5_HALF_LANES
        cols = slice(cp * pair, (cp + 1) * pair)
        y2 = [_dot(uc[hs][:, :depth], wint_ref[hs, :depth, cols]) + _dot_nt(sprev[hs], et_ref[hs, cols, :])
              for hs in range(2)]
        for i in range(2):
            m = 2 * cp + i
            tokens = halves(y2[0][:, i * LANES:(i + 1) * LANES], y2[1][:, i * LANES:(i + 1) * LANES])
            for t, y in zip((2 * m, 2 * m + 1), tokens):
                yt = _gelu(y + d * u[t])
                off = 0
                for rows in slab(t):
                    y_ref[rows, :] = yt[off:off + rows.size, :]
                    off += rows.size


def _s5(h, re0, im0, klag, f_tab, et_tab, a_t, d_skip, nseq, seq_rows, blocks_per_seq):
    tokens = h.shape[0]
    nb = re0.shape[0]
    ntile = D_MODEL // LANES
    half = S5_TILE_STATES
    rows = nseq * seq_rows
    nrow = rows // S5_T
    npow = -(-(nrow // S5_SEGMENTS + 1) // 8) * 8
    hspec = pl.BlockSpec((rows, LANES), lambda k, b, j: (b * blocks_per_seq + j, k))
    sspec = pl.BlockSpec((None, nseq, half), lambda k, b, j: (b, 0, k))
    per_tile = lambda shape: pl.BlockSpec((None,) + shape, lambda k, b, j: (k,) + (0,) * len(shape))
    kern = functools.partial(_s5_kernel, nseq=nseq, seq_rows=seq_rows)
    return pl.pallas_call(
        kern,
        grid=(ntile, nb, blocks_per_seq),
        in_specs=[hspec, sspec, sspec,
                  per_tile((S5_T, LANES, LANES)), per_tile(S5_HALF_TABLE), per_tile(S5_HALF_TABLE),
                  per_tile((2, 2 * half)),
                  pl.BlockSpec((1, LANES), lambda k, b, j: (0, k))],
        out_specs=[hspec, sspec, sspec],
        out_shape=[jax.ShapeDtypeStruct((tokens, D_MODEL), F32),
                   jax.ShapeDtypeStruct(re0.shape, F32), jax.ShapeDtypeStruct(im0.shape, F32)],
        scratch_shapes=[pltpu.VMEM((2, S5_T * S5_HALF_LANES, S5_T * S5_HALF_LANES), BF16),
                        pltpu.VMEM((2 * half // LANES, nrow, LANES), F32),
                        pltpu.VMEM((2 * half // LANES, nrow, LANES), F32),
                        pltpu.VMEM((nseq, 2 * half), F32),
                        pltpu.VMEM((npow, 2 * half), F32),
                        pltpu.VMEM((npow, 2 * half), F32)],
        compiler_params=_params("arbitrary", "arbitrary", "arbitrary"),
        name="s5",
    )(h, re0, im0, klag, f_tab, et_tab, a_t, d_skip)


def kernel(x_prompt, x_sample, cache_sb_k, cache_sb_v, state_ssm_re, state_ssm_im, norm_mix, norm_ffn,
           norm_final, ab_w_in, sgu_w, sgu_b, ab_w_out, ssm_lam_re, ssm_lam_im, ssm_log_step, ssm_b_re,
           ssm_b_im, ssm_c_re, ssm_c_im, ssm_d, ssm_w_glu, ffn_w_gate, ffn_w_up, ffn_w_down):
    bsz, seq, _ = x_prompt.shape
    dbsz, dseq, _ = x_sample.shape
    past = cache_sb_k.shape[2]
    heads = SB_WIDTH // HEAD_DIM
    row = lambda v: v.reshape(1, -1)

    xp = x_prompt.reshape(bsz * seq, D_MODEL)
    xs = x_sample.reshape(dbsz * dseq, D_MODEL)

    w_in = ab_w_in[0].astype(BF16)
    w_out = ab_w_out[0].astype(BF16)
    bs_rows = jnp.repeat(sgu_b[0].T, SGU_WIDTH // SGU_GROUPS, axis=1)
    g_mix0 = row(norm_mix[0])

    qp, kp, vp, kpb, vpb, up, gp = _proj(xp, g_mix0, w_in)
    qs, ks, vs, ksb, vsb, us, gs = _proj(xs, g_mix0, w_in)
    att_p = _sb_prompt(qp, kpb, vpb, bsz, seq)
    to_hdp = lambda c: jnp.transpose(c[0], (0, 2, 3, 1))
    att_s = _sb_sample(qs, ksb, vsb, to_hdp(cache_sb_k), to_hdp(cache_sb_v), dbsz, dseq)
    ffn_w = [(ffn_w_gate[l].astype(BF16), ffn_w_up[l].astype(BF16), ffn_w_down[l].astype(BF16)) for l in range(2)]
    g_mix1 = row(norm_mix[1])
    xp, hp = _mix_ffn(xp, att_p, up, gp, sgu_w[0], bs_rows, w_out, row(norm_ffn[0]), ffn_w[0], g_mix1)
    xs, hs = _mix_ffn(xs, att_s, us, gs, sgu_w[0][:, :dseq, :dseq], bs_rows[:dseq], w_out,
                      row(norm_ffn[0]), ffn_w[0], g_mix1)

    f_tab, et_tab, klag, a_t = _s5_tables(ssm_lam_re[0], ssm_lam_im[0], ssm_log_step[0], ssm_b_re[0],
                                          ssm_b_im[0], ssm_c_re[0], ssm_c_im[0])
    d_skip = row(ssm_d[0])
    nstates = state_ssm_re.shape[2] * state_ssm_re.shape[3]
    zeros = jnp.zeros((bsz, 1, nstates), F32)
    prompt_block = 4096
    yp, rp, ip = _s5(hp, zeros, zeros, klag, f_tab, et_tab, a_t, d_skip, 1, prompt_block, seq // prompt_block)
    ys, rs, is_ = _s5(hs, state_ssm_re[0].reshape(1, dbsz, nstates), state_ssm_im[0].reshape(1, dbsz, nstates),
                      klag, f_tab, et_tab, a_t, d_skip, dbsz, dseq, 1)
    w_glu = ssm_w_glu[0].astype(BF16)
    y_prompt = _glu_ffn(xp, yp, w_glu, row(norm_ffn[1]), ffn_w[1], row(norm_final))
    y_sample = _glu_ffn(xs, ys, w_glu, row(norm_ffn[1]), ffn_w[1], row(norm_final))

    state_shape = state_ssm_re.shape[2:]
    return (y_prompt.reshape(bsz, seq, D_MODEL), y_sample.reshape(dbsz, dseq, D_MODEL),
            kp.reshape(1, bsz, seq, heads, HEAD_DIM), vp.reshape(1, bsz, seq, heads, HEAD_DIM),
            ks.reshape(1, dbsz, dseq, heads, HEAD_DIM), vs.reshape(1, dbsz, dseq, heads, HEAD_DIM),
            gs.reshape(1, dbsz, dseq, SGU_WIDTH),
            rp.reshape((1, bsz) + state_shape), ip.reshape((1, bsz) + state_shape),
            rs.reshape((1, dbsz) + state_shape), is_.reshape((1, dbsz) + state_shape))
```

```python
import functools
import math

import jax
import jax.numpy as jnp
from jax import lax
from jax.experimental import pallas as pl
from jax.experimental.pallas import tpu as pltpu

F32 = jnp.float32
BF16 = jnp.bfloat16

LANES = 128
VMEM_LIMIT = 56 * 1024 * 1024

D_MODEL = 1024
HEAD_DIM = 64
SB_WIDTH = 512
SGU_WIDTH = 512
SGU_GROUPS = 8
SGU_CHUNK = 128
SSM_GROUP = 16
SSM_STATE = 64
FFN_HIDDEN = 2816
RMS_EPS = 1e-6

TOKEN_TILE = 512
SB_BLOCK = 128
SB_HEADS = SB_WIDTH // HEAD_DIM
SB_PAIRS = SB_WIDTH // LANES
SB_CUTOFF = -104.0
SB_PROMPT_FUSED_BLOCKS = 3
S5_T = 16
S5_SCAN_UNROLL = 8
S5_SEGMENTS = 8
S5_GROUPS_PER_TILE = LANES // SSM_GROUP
S5_TILE_STATES = S5_GROUPS_PER_TILE * SSM_STATE
S5_HALF_LANES = LANES // 2
S5_HALF_STATES = S5_TILE_STATES // 2
S5_HALF_TABLE = (2, S5_T * S5_HALF_LANES, 2 * S5_HALF_STATES)


def _rmsnorm(x, g):
    return x * lax.rsqrt(jnp.mean(x * x, axis=-1, keepdims=True) + RMS_EPS) * g


def _gelu(x):
    return 0.5 * x * (1.0 + jnp.tanh(math.sqrt(2.0 / math.pi) * (x + 0.044715 * (x * x * x))))


def _sigmoid(x):
    return 1.0 / (1.0 + jnp.exp(-x))


def _dot(a, b):
    return jnp.dot(a, b, preferred_element_type=F32)


def _dot_nt(a, b):
    return lax.dot_general(a, b, (((1,), (1,)), ((), ())), preferred_element_type=F32)


def _params(*sem):
    return pltpu.CompilerParams(dimension_semantics=sem, vmem_limit_bytes=VMEM_LIMIT)


def _resident(shape):
    nd = len(shape)
    return pl.BlockSpec(shape, lambda *_: (0,) * nd, pipeline_mode=pl.Buffered(1))


def _proj_kernel(x_ref, g_ref, w_ref, q_ref, k_ref, v_ref, kb_ref, vb_ref, u_ref, gv_ref):
    h = _rmsnorm(x_ref[...], g_ref[...]).astype(BF16)
    col = lambda i: _dot(h, w_ref[:, i * SB_WIDTH:(i + 1) * SB_WIDTH])
    q_ref[...] = (col(0) * (HEAD_DIM ** -0.5)).astype(BF16)
    for f32_ref, bf16_ref, z in ((k_ref, kb_ref, col(1)), (v_ref, vb_ref, col(2))):
        bf16_ref[...] = z.astype(BF16)
        f32_ref[...] = z.reshape(f32_ref.shape)
    u_ref[...] = _gelu(col(3))
    gv_ref[...] = _gelu(col(4))


def _proj(x, g, w_bf16):
    m = x.shape[0]
    tile = lambda w: pl.BlockSpec((TOKEN_TILE, w), lambda i: (i, 0))
    heads = pl.BlockSpec((TOKEN_TILE, SB_HEADS, HEAD_DIM), lambda i: (i, 0, 0))
    flat = lambda dt: jax.ShapeDtypeStruct((m, SB_WIDTH), dt)
    by_head = jax.ShapeDtypeStruct((m, SB_HEADS, HEAD_DIM), F32)
    return pl.pallas_call(
        _proj_kernel,
        grid=(m // TOKEN_TILE,),
        in_specs=[tile(D_MODEL), _resident((1, D_MODEL)), _resident(w_bf16.shape)],
        out_specs=[tile(SB_WIDTH), heads, heads] + [tile(SB_WIDTH)] * 4,
        out_shape=[flat(BF16), by_head, by_head, flat(BF16), flat(BF16), flat(F32), flat(F32)],
        compiler_params=_params("parallel"),
        name="proj",
    )(x, g, w_bf16)


def _suffix_and_total():
    j = lax.broadcasted_iota(jnp.int32, (SB_BLOCK, 2 * SB_BLOCK), 0)
    s = lax.broadcasted_iota(jnp.int32, (SB_BLOCK, 2 * SB_BLOCK), 1)
    return jnp.where(jnp.logical_or(j > s, s >= SB_BLOCK), 1.0, 0.0).astype(BF16)


def _sb_weights(z, c, sums_mat, masks):
    rows = z.shape[0]
    nb = len(masks)
    log_beta = jnp.minimum(z, 0.0) - jnp.log(1.0 + jnp.exp(-jnp.abs(z)))
    log_stay = log_beta - z
    parts = []
    for j, mask in enumerate(masks):
        stay = log_stay[:, j * SB_BLOCK:(j + 1) * SB_BLOCK]
        if mask is not None:
            stay = jnp.where(mask, stay, 0.0)
        parts.append(stay.astype(BF16))
    sums = _dot(parts[0] if nb == 1 else jnp.concatenate(parts, axis=0), sums_mat)
    ws = [None] * nb
    for j in reversed(range(nb)):
        s = sums[j * rows:(j + 1) * rows]
        w = jnp.exp(log_beta[:, j * SB_BLOCK:(j + 1) * SB_BLOCK] + s[:, :SB_BLOCK] + c)
        if masks[j] is not None:
            w = jnp.where(masks[j], w, 0.0)
        ws[j] = w.astype(BF16)
        c = c + s[:, SB_BLOCK:]
    return (ws[0] if nb == 1 else jnp.concatenate(ws, axis=1)), c


def _sb_alive(c):
    return jnp.max(c) > SB_CUTOFF


def _sb_pair_block(q_ref, o_ref, sums_mat):
    nq = q_ref.shape[0]
    lane = lax.broadcasted_iota(jnp.int32, (nq, LANES), 1)
    q_pairs = []
    for p in range(SB_PAIRS):
        q = q_ref[:, p * LANES:(p + 1) * LANES]
        zero = jnp.zeros_like(q)
        q_pairs.append(jnp.concatenate([jnp.where(lane < HEAD_DIM, q, zero),
                                        jnp.where(lane < HEAD_DIM, zero, q)], axis=0))

    def block(kblk, vblk, c, masks, first, transposed=False):
        qk = _dot if transposed else _dot_nt
        pv_dot = _dot_nt if transposed else _dot
        z = jnp.concatenate([qk(q_pairs[p], kblk[p]) for p in range(SB_PAIRS)], axis=0)
        w, c = _sb_weights(z, c, sums_mat, masks)
        for p in range(SB_PAIRS):
            cols = slice(p * LANES, (p + 1) * LANES)
            pv = pv_dot(w[2 * p * nq:(2 * p + 2) * nq], vblk[p])
            merged = jnp.where(lane < HEAD_DIM, pv[:nq], pv[nq:])
            o_ref[:, cols] = merged if first else o_ref[:, cols] + merged
        return c

    return block


def _pair_cols(x):
    return [x[:, p * LANES:(p + 1) * LANES] for p in range(SB_PAIRS)]


def _sb_causal(nq):
    row = lax.broadcasted_iota(jnp.int32, (SB_HEADS * nq, SB_BLOCK), 0)
    col = lax.broadcasted_iota(jnp.int32, (SB_HEADS * nq, SB_BLOCK), 1)
    return col < (row & (nq - 1))


def _sb_prompt_kernel(q_ref, k_ref, v_ref, o_ref):
    qb = pl.program_id(1)
    block = _sb_pair_block(q_ref, o_ref, _suffix_and_total())
    causal = _sb_causal(SB_BLOCK)
    c_init = jnp.zeros((SB_HEADS * SB_BLOCK, SB_BLOCK), F32)

    def newest(nb):
        rows = pl.ds(pl.multiple_of((qb - (nb - 1)) * SB_BLOCK, SB_BLOCK), nb * SB_BLOCK)
        return block(_pair_cols(k_ref[rows, :]), _pair_cols(v_ref[rows, :]), c_init,
                     [None] * (nb - 1) + [causal], True)

    fused = SB_PROMPT_FUSED_BLOCKS
    c = lax.cond(qb >= fused - 1, lambda: newest(fused), lambda: newest(1))

    def cond(state):
        kb, c = state
        return jnp.logical_and(kb >= 0, _sb_alive(c))

    def body(state):
        kb, c = state
        rows = pl.ds(pl.multiple_of(kb * SB_BLOCK, SB_BLOCK), SB_BLOCK)
        return kb - 1, block(_pair_cols(k_ref[rows, :]), _pair_cols(v_ref[rows, :]), c, [None], False)

    lax.while_loop(cond, body, (jnp.where(qb >= fused - 1, qb - fused, qb - 1), c))


def _sb_prompt(q, k, v, bsz, seq):
    nq = seq // SB_BLOCK
    qspec = pl.BlockSpec((SB_BLOCK, SB_WIDTH), lambda b, i: (b * nq + i, 0))
    kvspec = pl.BlockSpec((seq, SB_WIDTH), lambda b, i: (b, 0), pipeline_mode=pl.Buffered(1))
    return pl.pallas_call(
        _sb_prompt_kernel,
        grid=(bsz, nq),
        in_specs=[qspec, kvspec, kvspec],
        out_specs=qspec,
        out_shape=jax.ShapeDtypeStruct((bsz * seq, SB_WIDTH), F32),
        compiler_params=_params("parallel", "arbitrary"),
        name="sb_prompt",
    )(q, k, v)


def _sb_sample_kernel(q_ref, k_ref, v_ref, ck_hbm, cv_hbm, o_ref, kbuf, vbuf, sem):
    b = pl.program_id(0)
    nq = q_ref.shape[0]
    last = ck_hbm.shape[3] // SB_BLOCK - 1

    def fetch(kb, slot):
        pos = pl.ds(pl.multiple_of(kb * SB_BLOCK, SB_BLOCK), SB_BLOCK)
        return (pltpu.make_async_copy(ck_hbm.at[b, :, :, pos], kbuf.at[slot], sem.at[0, slot]),
                pltpu.make_async_copy(cv_hbm.at[b, :, :, pos], vbuf.at[slot], sem.at[1, slot]))

    def pair_rows(buf, slot):
        return [buf[slot, 2 * p:2 * p + 2].reshape(2 * HEAD_DIM, SB_BLOCK).astype(BF16) for p in range(SB_PAIRS)]

    for cp in fetch(last, last % 2):
        cp.start()

    sums_mat = _suffix_and_total()
    pad = jnp.zeros((SB_BLOCK - k_ref.shape[0], SB_WIDTH), BF16)
    new_k = jnp.concatenate([k_ref[...], pad], axis=0)
    new_v = jnp.concatenate([v_ref[...], pad], axis=0)
    block = _sb_pair_block(q_ref, o_ref, sums_mat)
    c = block(_pair_cols(new_k), _pair_cols(new_v), jnp.zeros((SB_HEADS * nq, SB_BLOCK), F32),
              [_sb_causal(nq)], True)

    def cond(state):
        kb, c = state
        return jnp.logical_and(kb >= 0, _sb_alive(c))

    def body(state):
        kb, c = state
        slot = kb % 2
        for cp in fetch(kb, slot):
            cp.wait()

        @pl.when(kb >= 1)
        def _():
            for cp in fetch(kb - 1, 1 - slot):
                cp.start()

        return kb - 1, block(pair_rows(kbuf, slot), pair_rows(vbuf, slot), c, [None], False, transposed=True)

    kb_end, _ = lax.while_loop(cond, body, (jnp.int32(last), c))

    @pl.when(kb_end >= 0)
    def _():
        for cp in fetch(kb_end, kb_end % 2):
            cp.wait()


def _sb_sample(q, k, v, cache_k, cache_v, bsz, n):
    new = pl.BlockSpec((n, SB_WIDTH), lambda b: (b, 0))
    old = pl.BlockSpec(memory_space=pl.ANY)
    slots = pltpu.VMEM((2, SB_HEADS, HEAD_DIM, SB_BLOCK), F32)
    return pl.pallas_call(
        _sb_sample_kernel,
        grid=(bsz,),
        in_specs=[new, new, new, old, old],
        out_specs=new,
        out_shape=jax.ShapeDtypeStruct((bsz * n, SB_WIDTH), F32),
        scratch_shapes=[slots, slots, pltpu.SemaphoreType.DMA((2, 2))],
        compiler_params=_params("arbitrary"),
        name="sb_sample",
    )(q, k, v, cache_k, cache_v)


FFN_CHUNK = 256


def _ffn_tail(x, g_ref, wg_ref, wu_ref, wd_ref, gn_ref, o_refs):
    h = _rmsnorm(x, g_ref[...]).astype(BF16)
    acc = x
    for c in range(FFN_HIDDEN // FFN_CHUNK):
        cols = slice(c * FFN_CHUNK, (c + 1) * FFN_CHUNK)
        gate = _dot(h, wg_ref[:, cols])
        up = _dot(h, wu_ref[:, cols])
        act = (gate * _sigmoid(gate) * up).astype(BF16)
        acc = acc + _dot(act, wd_ref[cols, :])
    if len(o_refs) == 2:
        o_refs[0][...] = acc
    o_refs[-1][...] = _rmsnorm(acc, gn_ref[...])


def _mix_ffn_kernel(x_ref, att_ref, u_ref, gv_ref, ws_ref, bs_ref, wo_ref,
                    g_ref, wg_ref, wu_ref, wd_ref, gn_ref, xo_ref, ho_ref, sg_ref):
    chunk = ws_ref.shape[1]
    t = lax.broadcasted_iota(jnp.int32, (chunk, chunk), 0)
    s = lax.broadcasted_iota(jnp.int32, (chunk, chunk), 1)
    tri = s <= t
    ws = [jnp.where(tri, ws_ref[g], 0.0).astype(BF16) for g in range(SGU_GROUPS)]
    lane = lax.broadcasted_iota(jnp.int32, (chunk, LANES), 1)
    group_dim = SGU_WIDTH // SGU_GROUPS
    for c in range(TOKEN_TILE // chunk):
        rows = slice(c * chunk, (c + 1) * chunk)
        for p in range(SGU_WIDTH // LANES):
            cols = slice(p * LANES, (p + 1) * LANES)
            gv = gv_ref[rows, cols].astype(BF16)
            mixed = jnp.where(lane < group_dim, _dot(ws[2 * p], gv), _dot(ws[2 * p + 1], gv)) + bs_ref[:, cols]
            sg_ref[rows, cols] = (u_ref[rows, cols] * mixed).astype(BF16)
    xo_ref[...] = (x_ref[...] + _dot(att_ref[...].astype(BF16), wo_ref[:SB_WIDTH, :])
                   + _dot(sg_ref[...], wo_ref[SB_WIDTH:, :]))
    _ffn_tail(xo_ref[...], g_ref, wg_ref, wu_ref, wd_ref, gn_ref, (xo_ref, ho_ref))


def _glu_ffn_kernel(x_ref, y_ref, w_ref, g_ref, wg_ref, wu_ref, wd_ref, gn_ref, ho_ref):
    y = y_ref[...].astype(BF16)
    for c in range(D_MODEL // FFN_CHUNK):
        cols = slice(c * FFN_CHUNK, (c + 1) * FFN_CHUNK)
        ga = _dot(y, w_ref[:, cols])
        gb = _dot(y, w_ref[:, D_MODEL + c * FFN_CHUNK:D_MODEL + (c + 1) * FFN_CHUNK])
        ho_ref[:, cols] = x_ref[:, cols] + ga * _sigmoid(gb)
    _ffn_tail(ho_ref[...], g_ref, wg_ref, wu_ref, wd_ref, gn_ref, (ho_ref,))


def _ffn_specs(g, ffn_w, g_next):
    wg, wu, wd = ffn_w
    return ([_resident((1, D_MODEL)), _resident(wg.shape), _resident(wu.shape), _resident(wd.shape),
             _resident((1, D_MODEL))], (g, wg, wu, wd, g_next))


def _mix_ffn(x, att, u, gv, ws, bs_rows, wo_bf16, g, ffn_w, g_next):
    m = x.shape[0]
    tile = lambda w: pl.BlockSpec((TOKEN_TILE, w), lambda i: (i, 0))
    ffn_specs, ffn_args = _ffn_specs(g, ffn_w, g_next)
    return pl.pallas_call(
        _mix_ffn_kernel,
        grid=(m // TOKEN_TILE,),
        in_specs=[tile(D_MODEL), tile(SB_WIDTH), tile(SGU_WIDTH), tile(SGU_WIDTH),
                  _resident(ws.shape), _resident(bs_rows.shape), _resident(wo_bf16.shape)] + ffn_specs,
        out_specs=[tile(D_MODEL)] * 2,
        out_shape=[jax.ShapeDtypeStruct((m, D_MODEL), F32)] * 2,
        scratch_shapes=[pltpu.VMEM((TOKEN_TILE, SGU_WIDTH), BF16)],
        compiler_params=_params("parallel"),
        name="mix_ffn",
    )(x, att, u, gv, ws, bs_rows, wo_bf16, *ffn_args)


def _glu_ffn(x, y, w_glu_bf16, g, ffn_w, g_next):
    m = x.shape[0]
    tile = pl.BlockSpec((TOKEN_TILE, D_MODEL), lambda i: (i, 0))
    ffn_specs, ffn_args = _ffn_specs(g, ffn_w, g_next)
    return pl.pallas_call(
        _glu_ffn_kernel,
        grid=(m // TOKEN_TILE,),
        in_specs=[tile, tile, _resident(w_glu_bf16.shape)] + ffn_specs,
        out_specs=tile,
        out_shape=jax.ShapeDtypeStruct((m, D_MODEL), F32),
        compiler_params=_params("parallel"),
        name="glu_ffn",
    )(x, y, w_glu_bf16, *ffn_args)


def _s5_tables_kernel(lr_ref, li_ref, ls_ref, bre_ref, bim_ref, cre_ref, cim_ref,
                      f_ref, et_ref, klag_ref, at_ref, f32_ref):
    half = S5_TILE_STATES
    lr, li = lr_ref[...], li_ref[...]
    step = jnp.exp(ls_ref[...])
    mag, ang = jnp.exp(lr * step), li * step
    ar, ai = mag * jnp.cos(ang), mag * jnp.sin(ang)
    den = lr * lr + li * li
    nr = ar - 1.0
    co_re = (nr * lr + ai * li) / den
    co_im = (ai * lr - nr * li) / den

    row_group = lax.broadcasted_iota(jnp.int32, (LANES, half), 0) // SSM_GROUP
    col_group = lax.broadcasted_iota(jnp.int32, (LANES, half), 1) // SSM_STATE

    def block_diag(ref):
        return jnp.where(row_group == col_group, jnp.concatenate([ref[...]] * S5_GROUPS_PER_TILE, axis=1), 0.0)

    b_re, b_im, c_re, c_im = (block_diag(r) for r in (bre_ref, bim_ref, cre_ref, cim_ref))
    g_re = co_re * b_re - co_im * b_im
    g_im = co_re * b_im + co_im * b_re
    p_re, p_im = ar, ai
    def put_halves(ref, block, re, im):
        for hs in range(2):
            lanes = slice(hs * S5_HALF_LANES, (hs + 1) * S5_HALF_LANES)
            states = slice(hs * S5_HALF_STATES, (hs + 1) * S5_HALF_STATES)
            rows = slice(block * S5_HALF_LANES, (block + 1) * S5_HALF_LANES)
            ref[hs, rows, :S5_HALF_STATES] = re[lanes, states].astype(BF16)
            ref[hs, rows, S5_HALF_STATES:] = im[lanes, states].astype(BF16)

    for tau in range(S5_T):
        rows = slice((S5_T - 1 - tau) * LANES, (S5_T - tau) * LANES)
        f32_ref[rows, :half] = g_re
        f32_ref[rows, half:] = g_im
        put_halves(f_ref, S5_T - 1 - tau, g_re, g_im)
        put_halves(et_ref, tau, c_re * p_re - c_im * p_im, -c_re * p_im - c_im * p_re)
        if tau < S5_T - 1:
            g_re, g_im = g_re * ar - g_im * ai, g_re * ai + g_im * ar
            p_re, p_im = p_re * ar - p_im * ai, p_re * ai + p_im * ar
    at_ref[0:1, :] = jnp.concatenate([p_re, p_re], axis=1)
    at_ref[1:2, :] = jnp.concatenate([-p_im, p_im], axis=1)

    f_all = f32_ref[...]
    f_hi = f_all.astype(BF16)
    f_lo = (f_all - f_hi.astype(F32)).astype(BF16)
    cc = jnp.concatenate([c_re, -c_im], axis=1)
    c_hi = cc.astype(BF16)
    c_lo = (cc - c_hi.astype(F32)).astype(BF16)
    k_all = _dot_nt(f_hi, c_hi) + _dot_nt(f_hi, c_lo) + _dot_nt(f_lo, c_hi)
    for tau in range(S5_T):
        rows = slice((S5_T - 1 - tau) * LANES, (S5_T - tau) * LANES)
        klag_ref[tau] = k_all[rows, :]


def _s5_tables(lam_re, lam_im, log_step, b_re, b_im, c_re, c_im):
    groups, nstate = lam_re.shape
    ntile = groups // S5_GROUPS_PER_TILE
    half = S5_TILE_STATES
    vec = lambda a: a.reshape(ntile, 1, half)
    mat = lambda a: a.reshape(ntile, LANES, nstate)
    args = (vec(lam_re), vec(lam_im), vec(jnp.repeat(log_step, nstate)),
            mat(jnp.swapaxes(b_re, 1, 2)), mat(jnp.swapaxes(b_im, 1, 2)), mat(c_re), mat(c_im))
    per_tile = lambda shape: pl.BlockSpec((None,) + shape, lambda k: (k,) + (0,) * len(shape))
    return pl.pallas_call(
        _s5_tables_kernel,
        grid=(ntile,),
        in_specs=[per_tile((1, half))] * 3 + [per_tile((LANES, nstate))] * 4,
        out_specs=[per_tile(S5_HALF_TABLE), per_tile(S5_HALF_TABLE),
                   per_tile((S5_T, LANES, LANES)), per_tile((2, 2 * half))],
        out_shape=[jax.ShapeDtypeStruct((ntile,) + S5_HALF_TABLE, BF16),
                   jax.ShapeDtypeStruct((ntile,) + S5_HALF_TABLE, BF16),
                   jax.ShapeDtypeStruct((ntile, S5_T, LANES, LANES), F32),
                   jax.ShapeDtypeStruct((ntile, 2, 2 * half), F32)],
        scratch_shapes=[pltpu.VMEM((S5_T * LANES, 2 * half), F32)],
        compiler_params=_params("parallel"),
        name="s5_tables",
    )(*args)


def _s5_kernel(h_ref, re0_ref, im0_ref, klag_ref, f_ref, et_ref, at_ref, d_ref,
               y_ref, reo_ref, imo_ref, wint_ref, xend_ref, sprev_ref, s_ref, pow1_ref, pow2_ref,
               *, nseq, seq_rows):
    first_of_tile = jnp.logical_and(pl.program_id(1) == 0, pl.program_id(2) == 0)
    nchunk = seq_rows // S5_T
    half = S5_TILE_STATES

    def halves(a, b):
        lane = lax.broadcasted_iota(jnp.int32, a.shape, 1)
        first = lane < S5_HALF_LANES
        return (jnp.where(first, a, pltpu.roll(b, S5_HALF_LANES, 1)),
                jnp.where(first, pltpu.roll(a, S5_HALF_LANES, 1), b))

    @pl.when(first_of_tile)
    def _():
        wint_ref[...] = jnp.zeros_like(wint_ref)
        zero = jnp.zeros((LANES, LANES), F32)
        for j in range(S5_T):
            for m in range(j // 2, S5_T // 2):
                lag = 2 * m - j
                low, high = halves(klag_ref[lag] if lag >= 0 else zero, klag_ref[lag + 1])
                rows = slice(j * S5_HALF_LANES, (j + 1) * S5_HALF_LANES)
                wint_ref[0, rows, m * LANES:(m + 1) * LANES] = low[:S5_HALF_LANES].astype(BF16)
                wint_ref[1, rows, m * LANES:(m + 1) * LANES] = high[S5_HALF_LANES:].astype(BF16)

    @pl.when(pl.program_id(2) == 0)
    def _():
        s_ref[:, :half] = re0_ref[...]
        s_ref[:, half:] = im0_ref[...]

    def slab(t):
        if nseq == 1:
            return [pl.ds(t, nchunk, stride=S5_T)]
        return [pl.ds(n * S5_T + t, nseq, stride=seq_rows) for n in range(nchunk)]

    u = []
    for t in range(S5_T):
        parts = [h_ref[rows, :] for rows in slab(t)]
        u.append(parts[0] if len(parts) == 1 else jnp.concatenate(parts, axis=0))
    split = [halves(u[2 * m], u[2 * m + 1]) for m in range(S5_T // 2)]
    uc = [jnp.concatenate([pair[hs].astype(BF16) for pair in split], axis=1) for hs in range(2)]

    nslab = 2 * half // LANES

    def load_rows(ref, rows):
        return jnp.concatenate([ref[q, rows, :] for q in range(nslab)], axis=1)

    def store_rows(ref, rows, val, add=False):
        for q in range(nslab):
            piece = val[:, q * LANES:(q + 1) * LANES]
            ref[q, rows, :] = ref[q, rows, :] + piece if add else piece

    hstates = S5_HALF_STATES
    hslabs = hstates // LANES
    for hs in range(2):
        x = _dot(uc[hs], f_ref[hs])
        for part in range(2):
            for q in range(hslabs):
                col = part * hstates + q * LANES
                xend_ref[part * (nslab // 2) + hs * hslabs + q] = x[:, col:col + LANES]

    a1 = at_ref[0:1, :]
    a2 = at_ref[1:2, :]
    swap = lambda s: jnp.concatenate([s[:, half:], s[:, :half]], axis=1)
    cmul = lambda p1, p2, s: p1 * s + p2 * swap(s)

    if nseq > 1:
        def step(n, s):
            rows = pl.ds(n * nseq, nseq)
            store_rows(sprev_ref, rows, s)
            return cmul(a1, a2, s) + load_rows(xend_ref, rows)

        s_fin = lax.fori_loop(0, nchunk, step, s_ref[...], unroll=min(S5_SCAN_UNROLL, nchunk))
    else:
        seg_len = nchunk // S5_SEGMENTS

        @pl.when(first_of_tile)
        def _():
            z = jnp.concatenate([jnp.ones((1, half), F32), jnp.zeros((1, half), F32)], axis=1)
            for i in range(seg_len + 1):
                pow1_ref[i:i + 1, :] = jnp.concatenate([z[:, :half], z[:, :half]], axis=1)
                pow2_ref[i:i + 1, :] = jnp.concatenate([-z[:, half:], z[:, half:]], axis=1)
                z = cmul(a1, a2, z)

        def local_step(i, s):
            rows = pl.ds(i, S5_SEGMENTS, stride=seg_len)
            store_rows(sprev_ref, rows, s)
            return cmul(a1, a2, s) + load_rows(xend_ref, rows)

        local_end = lax.fori_loop(0, seg_len, local_step, jnp.zeros((S5_SEGMENTS, 2 * half), F32),
                                  unroll=S5_SCAN_UNROLL)
        seg1 = pow1_ref[seg_len:seg_len + 1, :]
        seg2 = pow2_ref[seg_len:seg_len + 1, :]
        start = s_ref[...]
        for seg in range(S5_SEGMENTS):
            rows = slice(seg * seg_len, (seg + 1) * seg_len)
            store_rows(sprev_ref, rows, pow1_ref[:seg_len, :] * start + pow2_ref[:seg_len, :] * swap(start),
                       add=True)
            start = cmul(seg1, seg2, start) + local_end[seg:seg + 1, :]
        s_fin = start
    s_ref[...] = s_fin

    @pl.when(pl.program_id(2) == pl.num_programs(2) - 1)
    def _():
        reo_ref[...] = s_fin[:, :half]
        imo_ref[...] = s_fin[:, half:]

    sprev = [jnp.concatenate([sprev_ref[part * (nslab // 2) + hs * hslabs + q]
                              for part in range(2) for q in range(hslabs)], axis=1).astype(BF16)
             for hs in range(2)]
    d = d_ref[...]
    pair = 2 * LANES
    for cp in range(S5_T // 4):
        depth = (4 * cp + 4) * S5_HALF_LANES
        cols = slice(cp * pair, (cp + 1) * pair)
        y2 = [_dot(uc[hs][:, :depth], wint_ref[hs, :depth, cols]) + _dot_nt(sprev[hs], et_ref[hs, cols, :])
              for hs in range(2)]
        for i in range(2):
            m = 2 * cp + i
            tokens = halves(y2[0][:, i * LANES:(i + 1) * LANES], y2[1][:, i * LANES:(i + 1) * LANES])
            for t, y in zip((2 * m, 2 * m + 1), tokens):
                yt = _gelu(y + d * u[t])
                off = 0
                for rows in slab(t):
                    y_ref[rows, :] = yt[off:off + rows.size, :]
                    off += rows.size


def _s5(h, re0, im0, klag, f_tab, et_tab, a_t, d_skip, nseq, seq_rows, blocks_per_seq):
    tokens = h.shape[0]
    nb = re0.shape[0]
    ntile = D_MODEL // LANES
    half = S5_TILE_STATES
    rows = nseq * seq_rows
    nrow = rows // S5_T
    npow = -(-(nrow // S5_SEGMENTS + 1) // 8) * 8
    hspec = pl.BlockSpec((rows, LANES), lambda k, b, j: (b * blocks_per_seq + j, k))
    sspec = pl.BlockSpec((None, nseq, half), lambda k, b, j: (b, 0, k))
    per_tile = lambda shape: pl.BlockSpec((None,) + shape, lambda k, b, j: (k,) + (0,) * len(shape))
    kern = functools.partial(_s5_kernel, nseq=nseq, seq_rows=seq_rows)
    return pl.pallas_call(
        kern,
        grid=(ntile, nb, blocks_per_seq),
        in_specs=[hspec, sspec, sspec,
                  per_tile((S5_T, LANES, LANES)), per_tile(S5_HALF_TABLE), per_tile(S5_HALF_TABLE),
                  per_tile((2, 2 * half)),
                  pl.BlockSpec((1, LANES), lambda k, b, j: (0, k))],
        out_specs=[hspec, sspec, sspec],
        out_shape=[jax.ShapeDtypeStruct((tokens, D_MODEL), F32),
                   jax.ShapeDtypeStruct(re0.shape, F32), jax.ShapeDtypeStruct(im0.shape, F32)],
        scratch_shapes=[pltpu.VMEM((2, S5_T * S5_HALF_LANES, S5_T * S5_HALF_LANES), BF16),
                        pltpu.VMEM((2 * half // LANES, nrow, LANES), F32),
                        pltpu.VMEM((2 * half // LANES, nrow, LANES), F32),
                        pltpu.VMEM((nseq, 2 * half), F32),
                        pltpu.VMEM((npow, 2 * half), F32),
                        pltpu.VMEM((npow, 2 * half), F32)],
        compiler_params=_params("arbitrary", "arbitrary", "arbitrary"),
        name="s5",
    )(h, re0, im0, klag, f_tab, et_tab, a_t, d_skip)


def kernel(x_prompt, x_sample, cache_sb_k, cache_sb_v, state_ssm_re, state_ssm_im, norm_mix, norm_ffn,
           norm_final, ab_w_in, sgu_w, sgu_b, ab_w_out, ssm_lam_re, ssm_lam_im, ssm_log_step, ssm_b_re,
           ssm_b_im, ssm_c_re, ssm_c_im, ssm_d, ssm_w_glu, ffn_w_gate, ffn_w_up, ffn_w_down):
    bsz, seq, _ = x_prompt.shape
    dbsz, dseq, _ = x_sample.shape
    past = cache_sb_k.shape[2]
    heads = SB_WIDTH // HEAD_DIM
    row = lambda v: v.reshape(1, -1)

    xp = x_prompt.reshape(bsz * seq, D_MODEL)
    xs = x_sample.reshape(dbsz * dseq, D_MODEL)

    w_in = ab_w_in[0].astype(BF16)
    w_out = ab_w_out[0].astype(BF16)
    bs_rows = jnp.repeat(sgu_b[0].T, SGU_WIDTH // SGU_GROUPS, axis=1)
    g_mix0 = row(norm_mix[0])

    qp, kp, vp, kpb, vpb, up, gp = _proj(xp, g_mix0, w_in)
    qs, ks, vs, ksb, vsb, us, gs = _proj(xs, g_mix0, w_in)
    att_p = _sb_prompt(qp, kpb, vpb, bsz, seq)
    to_hdp = lambda c: jnp.transpose(c[0], (0, 2, 3, 1))
    att_s = _sb_sample(qs, ksb, vsb, to_hdp(cache_sb_k), to_hdp(cache_sb_v), dbsz, dseq)
    ffn_w = [(ffn_w_gate[l].astype(BF16), ffn_w_up[l].astype(BF16), ffn_w_down[l].astype(BF16)) for l in range(2)]
    g_mix1 = row(norm_mix[1])
    xp, hp = _mix_ffn(xp, att_p, up, gp, sgu_w[0], bs_rows, w_out, row(norm_ffn[0]), ffn_w[0], g_mix1)
    xs, hs = _mix_ffn(xs, att_s, us, gs, sgu_w[0][:, :dseq, :dseq], bs_rows[:dseq], w_out,
                      row(norm_ffn[0]), ffn_w[0], g_mix1)

    f_tab, et_tab, klag, a_t = _s5_tables(ssm_lam_re[0], ssm_lam_im[0], ssm_log_step[0], ssm_b_re[0],
                                          ssm_b_im[0], ssm_c_re[0], ssm_c_im[0])
    d_skip = row(ssm_d[0])
    nstates = state_ssm_re.shape[2] * state_ssm_re.shape[3]
    zeros = jnp.zeros((bsz, 1, nstates), F32)
    prompt_block = 4096
    yp, rp, ip = _s5(hp, zeros, zeros, klag, f_tab, et_tab, a_t, d_skip, 1, prompt_block, seq // prompt_block)
    ys, rs, is_ = _s5(hs, state_ssm_re[0].reshape(1, dbsz, nstates), state_ssm_im[0].reshape(1, dbsz, nstates),
                      klag, f_tab, et_tab, a_t, d_skip, dbsz, dseq, 1)
    w_glu = ssm_w_glu[0].astype(BF16)
    y_prompt = _glu_ffn(xp, yp, w_glu, row(norm_ffn[1]), ffn_w[1], row(norm_final))
    y_sample = _glu_ffn(xs, ys, w_glu, row(norm_ffn[1]), ffn_w[1], row(norm_final))

    state_shape = state_ssm_re.shape[2:]
    return (y_prompt.reshape(bsz, seq, D_MODEL), y_sample.reshape(dbsz, dseq, D_MODEL),
            kp.reshape(1, bsz, seq, heads, HEAD_DIM), vp.reshape(1, bsz, seq, heads, HEAD_DIM),
            ks.reshape(1, dbsz, dseq, heads, HEAD_DIM), vs.reshape(1, dbsz, dseq, heads, HEAD_DIM),
            gs.reshape(1, dbsz, dseq, SGU_WIDTH),
            rp.reshape((1, bsz) + state_shape), ip.reshape((1, bsz) + state_shape),
            rs.reshape((1, dbsz) + state_shape), is_.reshape((1, dbsz) + state_shape))
```

```python
import functools
import math

import jax
import jax.numpy as jnp
from jax import lax
from jax.experimental import pallas as pl
from jax.experimental.pallas import tpu as pltpu

F32 = jnp.float32
BF16 = jnp.bfloat16

LANES = 128
VMEM_LIMIT = 56 * 1024 * 1024

D_MODEL = 1024
HEAD_DIM = 64
SB_WIDTH = 512
SGU_WIDTH = 512
SGU_GROUPS = 8
SGU_CHUNK = 128
SSM_GROUP = 16
SSM_STATE = 64
FFN_HIDDEN = 2816
RMS_EPS = 1e-6

TOKEN_TILE = 512
SB_BLOCK = 128
SB_HEADS = SB_WIDTH // HEAD_DIM
SB_PAIRS = SB_WIDTH // LANES
SB_CUTOFF = -104.0
SB_PROMPT_FUSED_BLOCKS = 3
S5_T = 16
S5_SCAN_UNROLL = 8
S5_SEGMENTS = 8
S5_GROUPS_PER_TILE = LANES // SSM_GROUP
S5_TILE_STATES = S5_GROUPS_PER_TILE * SSM_STATE
S5_HALF_LANES = LANES // 2
S5_HALF_STATES = S5_TILE_STATES // 2
S5_HALF_TABLE = (2, S5_T * S5_HALF_LANES, 2 * S5_HALF_STATES)


def _rmsnorm(x, g):
    return x * lax.rsqrt(jnp.mean(x * x, axis=-1, keepdims=True) + RMS_EPS) * g


def _gelu(x):
    return 0.5 * x * (1.0 + jnp.tanh(math.sqrt(2.0 / math.pi) * (x + 0.044715 * (x * x * x))))


def _sigmoid(x):
    return 1.0 / (1.0 + jnp.exp(-x))


def _dot(a, b):
    return jnp.dot(a, b, preferred_element_type=F32)


def _dot_nt(a, b):
    return lax.dot_general(a, b, (((1,), (1,)), ((), ())), preferred_element_type=F32)


def _params(*sem):
    return pltpu.CompilerParams(dimension_semantics=sem, vmem_limit_bytes=VMEM_LIMIT)


def _resident(shape):
    nd = len(shape)
    return pl.BlockSpec(shape, lambda *_: (0,) * nd, pipeline_mode=pl.Buffered(1))


def _proj_kernel(x_ref, g_ref, w_ref, q_ref, k_ref, v_ref, kb_ref, vb_ref, u_ref, gv_ref):
    h = _rmsnorm(x_ref[...], g_ref[...]).astype(BF16)
    col = lambda i: _dot(h, w_ref[:, i * SB_WIDTH:(i + 1) * SB_WIDTH])
    for f32_ref, bf16_ref, z in ((k_ref, kb_ref, col(1)), (v_ref, vb_ref, col(2))):
        bf16_ref[...] = z.astype(BF16)
        f32_ref[...] = z.reshape(f32_ref.shape)
    q_ref[...] = (col(0) * (HEAD_DIM ** -0.5)).astype(BF16)
    u_ref[...] = _gelu(col(3))
    gv_ref[...] = _gelu(col(4))


def _proj(x, g, w_bf16):
    m = x.shape[0]
    tile = lambda w: pl.BlockSpec((TOKEN_TILE, w), lambda i: (i, 0))
    heads = pl.BlockSpec((TOKEN_TILE, SB_HEADS, HEAD_DIM), lambda i: (i, 0, 0))
    flat = lambda dt: jax.ShapeDtypeStruct((m, SB_WIDTH), dt)
    by_head = jax.ShapeDtypeStruct((m, SB_HEADS, HEAD_DIM), F32)
    return pl.pallas_call(
        _proj_kernel,
        grid=(m // TOKEN_TILE,),
        in_specs=[tile(D_MODEL), _resident((1, D_MODEL)), _resident(w_bf16.shape)],
        out_specs=[tile(SB_WIDTH), heads, heads] + [tile(SB_WIDTH)] * 4,
        out_shape=[flat(BF16), by_head, by_head, flat(BF16), flat(BF16), flat(F32), flat(F32)],
        compiler_params=_params("parallel"),
        name="proj",
    )(x, g, w_bf16)


def _suffix_and_total():
    j = lax.broadcasted_iota(jnp.int32, (SB_BLOCK, 2 * SB_BLOCK), 0)
    s = lax.broadcasted_iota(jnp.int32, (SB_BLOCK, 2 * SB_BLOCK), 1)
    return jnp.where(jnp.logical_or(j > s, s >= SB_BLOCK), 1.0, 0.0).astype(BF16)


def _sb_weights(z, c, sums_mat, masks):
    rows = z.shape[0]
    nb = len(masks)
    log_beta = jnp.minimum(z, 0.0) - jnp.log(1.0 + jnp.exp(-jnp.abs(z)))
    log_stay = log_beta - z
    parts = []
    for j, mask in enumerate(masks):
        stay = log_stay[:, j * SB_BLOCK:(j + 1) * SB_BLOCK]
        if mask is not None:
            stay = jnp.where(mask, stay, 0.0)
        parts.append(stay.astype(BF16))
    sums = _dot(parts[0] if nb == 1 else jnp.concatenate(parts, axis=0), sums_mat)
    ws = [None] * nb
    for j in reversed(range(nb)):
        s = sums[j * rows:(j + 1) * rows]
        w = jnp.exp(log_beta[:, j * SB_BLOCK:(j + 1) * SB_BLOCK] + s[:, :SB_BLOCK] + c)
        if masks[j] is not None:
            w = jnp.where(masks[j], w, 0.0)
        ws[j] = w.astype(BF16)
        c = c + s[:, SB_BLOCK:]
    return (ws[0] if nb == 1 else jnp.concatenate(ws, axis=1)), c


def _sb_alive(c):
    return jnp.max(c) > SB_CUTOFF


def _sb_pair_block(q_ref, o_ref, sums_mat):
    nq = q_ref.shape[0]
    lane = lax.broadcasted_iota(jnp.int32, (nq, LANES), 1)
    q_pairs = []
    for p in range(SB_PAIRS):
        q = q_ref[:, p * LANES:(p + 1) * LANES]
        zero = jnp.zeros_like(q)
        q_pairs.append(jnp.concatenate([jnp.where(lane < HEAD_DIM, q, zero),
                                        jnp.where(lane < HEAD_DIM, zero, q)], axis=0))

    def block(kblk, vblk, c, masks, first, transposed=False):
        qk = _dot if transposed else _dot_nt
        pv_dot = _dot_nt if transposed else _dot
        z = jnp.concatenate([qk(q_pairs[p], kblk[p]) for p in range(SB_PAIRS)], axis=0)
        w, c = _sb_weights(z, c, sums_mat, masks)
        for p in range(SB_PAIRS):
            cols = slice(p * LANES, (p + 1) * LANES)
            pv = pv_dot(w[2 * p * nq:(2 * p + 2) * nq], vblk[p])
            merged = jnp.where(lane < HEAD_DIM, pv[:nq], pv[nq:])
            o_ref[:, cols] = merged if first else o_ref[:, cols] + merged
        return c

    return block


def _pair_cols(x):
    return [x[:, p * LANES:(p + 1) * LANES] for p in range(SB_PAIRS)]


def _sb_causal(nq):
    row = lax.broadcasted_iota(jnp.int32, (SB_HEADS * nq, SB_BLOCK), 0)
    col = lax.broadcasted_iota(jnp.int32, (SB_HEADS * nq, SB_BLOCK), 1)
    return col < (row & (nq - 1))


def _sb_prompt_kernel(q_ref, k_ref, v_ref, o_ref):
    qb = pl.program_id(1)
    block = _sb_pair_block(q_ref, o_ref, _suffix_and_total())
    causal = _sb_causal(SB_BLOCK)
    c_init = jnp.zeros((SB_HEADS * SB_BLOCK, SB_BLOCK), F32)

    def newest(nb):
        rows = pl.ds(pl.multiple_of((qb - (nb - 1)) * SB_BLOCK, SB_BLOCK), nb * SB_BLOCK)
        return block(_pair_cols(k_ref[rows, :]), _pair_cols(v_ref[rows, :]), c_init,
                     [None] * (nb - 1) + [causal], True)

    fused = SB_PROMPT_FUSED_BLOCKS
    c = lax.cond(qb >= fused - 1, lambda: newest(fused), lambda: newest(1))

    def cond(state):
        kb, c = state
        return jnp.logical_and(kb >= 0, _sb_alive(c))

    def body(state):
        kb, c = state
        rows = pl.ds(pl.multiple_of(kb * SB_BLOCK, SB_BLOCK), SB_BLOCK)
        return kb - 1, block(_pair_cols(k_ref[rows, :]), _pair_cols(v_ref[rows, :]), c, [None], False)

    lax.while_loop(cond, body, (jnp.where(qb >= fused - 1, qb - fused, qb - 1), c))


def _sb_prompt(q, k, v, bsz, seq):
    nq = seq // SB_BLOCK
    qspec = pl.BlockSpec((SB_BLOCK, SB_WIDTH), lambda b, i: (b * nq + i, 0))
    kvspec = pl.BlockSpec((seq, SB_WIDTH), lambda b, i: (b, 0), pipeline_mode=pl.Buffered(1))
    return pl.pallas_call(
        _sb_prompt_kernel,
        grid=(bsz, nq),
        in_specs=[qspec, kvspec, kvspec],
        out_specs=qspec,
        out_shape=jax.ShapeDtypeStruct((bsz * seq, SB_WIDTH), F32),
        compiler_params=_params("parallel", "arbitrary"),
        name="sb_prompt",
    )(q, k, v)


def _sb_sample_kernel(q_ref, k_ref, v_ref, ck_hbm, cv_hbm, o_ref, kbuf, vbuf, sem):
    b = pl.program_id(0)
    nq = q_ref.shape[0]
    last = ck_hbm.shape[3] // SB_BLOCK - 1

    def fetch(kb, slot):
        pos = pl.ds(pl.multiple_of(kb * SB_BLOCK, SB_BLOCK), SB_BLOCK)
        return (pltpu.make_async_copy(ck_hbm.at[b, :, :, pos], kbuf.at[slot], sem.at[0, slot]),
                pltpu.make_async_copy(cv_hbm.at[b, :, :, pos], vbuf.at[slot], sem.at[1, slot]))

    def pair_rows(buf, slot):
        return [buf[slot, 2 * p:2 * p + 2].reshape(2 * HEAD_DIM, SB_BLOCK).astype(BF16) for p in range(SB_PAIRS)]

    for cp in fetch(last, last % 2):
        cp.start()

    sums_mat = _suffix_and_total()
    pad = jnp.zeros((SB_BLOCK - k_ref.shape[0], SB_WIDTH), BF16)
    new_k = jnp.concatenate([k_ref[...], pad], axis=0)
    new_v = jnp.concatenate([v_ref[...], pad], axis=0)
    block = _sb_pair_block(q_ref, o_ref, sums_mat)
    c = block(_pair_cols(new_k), _pair_cols(new_v), jnp.zeros((SB_HEADS * nq, SB_BLOCK), F32),
              [_sb_causal(nq)], True)

    def cond(state):
        kb, c = state
        return jnp.logical_and(kb >= 0, _sb_alive(c))

    def body(state):
        kb, c = state
        slot = kb % 2
        for cp in fetch(kb, slot):
            cp.wait()

        @pl.when(kb >= 1)
        def _():
            for cp in fetch(kb - 1, 1 - slot):
                cp.start()

        return kb - 1, block(pair_rows(kbuf, slot), pair_rows(vbuf, slot), c, [None], False, transposed=True)

    kb_end, _ = lax.while_loop(cond, body, (jnp.int32(last), c))

    @pl.when(kb_end >= 0)
    def _():
        for cp in fetch(kb_end, kb_end % 2):
            cp.wait()


def _sb_sample(q, k, v, cache_k, cache_v, bsz, n):
    new = pl.BlockSpec((n, SB_WIDTH), lambda b: (b, 0))
    old = pl.BlockSpec(memory_space=pl.ANY)
    slots = pltpu.VMEM((2, SB_HEADS, HEAD_DIM, SB_BLOCK), F32)
    return pl.pallas_call(
        _sb_sample_kernel,
        grid=(bsz,),
        in_specs=[new, new, new, old, old],
        out_specs=new,
        out_shape=jax.ShapeDtypeStruct((bsz * n, SB_WIDTH), F32),
        scratch_shapes=[slots, slots, pltpu.SemaphoreType.DMA((2, 2))],
        compiler_params=_params("arbitrary"),
        name="sb_sample",
    )(q, k, v, cache_k, cache_v)


FFN_CHUNK = 256


def _ffn_tail(x, g_ref, wg_ref, wu_ref, wd_ref, gn_ref, o_refs):
    h = _rmsnorm(x, g_ref[...]).astype(BF16)
    acc = x
    for c in range(FFN_HIDDEN // FFN_CHUNK):
        cols = slice(c * FFN_CHUNK, (c + 1) * FFN_CHUNK)
        gate = _dot(h, wg_ref[:, cols])
        up = _dot(h, wu_ref[:, cols])
        act = (gate * _sigmoid(gate) * up).astype(BF16)
        acc = acc + _dot(act, wd_ref[cols, :])
    if len(o_refs) == 2:
        o_refs[0][...] = acc
    o_refs[-1][...] = _rmsnorm(acc, gn_ref[...])


def _mix_ffn_kernel(x_ref, att_ref, u_ref, gv_ref, ws_ref, bs_ref, wo_ref,
                    g_ref, wg_ref, wu_ref, wd_ref, gn_ref, xo_ref, ho_ref, sg_ref):
    chunk = ws_ref.shape[1]
    t = lax.broadcasted_iota(jnp.int32, (chunk, chunk), 0)
    s = lax.broadcasted_iota(jnp.int32, (chunk, chunk), 1)
    tri = s <= t
    ws = [jnp.where(tri, ws_ref[g], 0.0).astype(BF16) for g in range(SGU_GROUPS)]
    lane = lax.broadcasted_iota(jnp.int32, (chunk, LANES), 1)
    group_dim = SGU_WIDTH // SGU_GROUPS
    for c in range(TOKEN_TILE // chunk):
        rows = slice(c * chunk, (c + 1) * chunk)
        for p in range(SGU_WIDTH // LANES):
            cols = slice(p * LANES, (p + 1) * LANES)
            gv = gv_ref[rows, cols].astype(BF16)
            mixed = jnp.where(lane < group_dim, _dot(ws[2 * p], gv), _dot(ws[2 * p + 1], gv)) + bs_ref[:, cols]
            sg_ref[rows, cols] = (u_ref[rows, cols] * mixed).astype(BF16)
    xo_ref[...] = (x_ref[...] + _dot(att_ref[...].astype(BF16), wo_ref[:SB_WIDTH, :])
                   + _dot(sg_ref[...], wo_ref[SB_WIDTH:, :]))
    _ffn_tail(xo_ref[...], g_ref, wg_ref, wu_ref, wd_ref, gn_ref, (xo_ref, ho_ref))


def _glu_ffn_kernel(x_ref, y_ref, w_ref, g_ref, wg_ref, wu_ref, wd_ref, gn_ref, ho_ref):
    y = y_ref[...].astype(BF16)
    for c in range(D_MODEL // FFN_CHUNK):
        cols = slice(c * FFN_CHUNK, (c + 1) * FFN_CHUNK)
        ga = _dot(y, w_ref[:, cols])
        gb = _dot(y, w_ref[:, D_MODEL + c * FFN_CHUNK:D_MODEL + (c + 1) * FFN_CHUNK])
        ho_ref[:, cols] = x_ref[:, cols] + ga * _sigmoid(gb)
    _ffn_tail(ho_ref[...], g_ref, wg_ref, wu_ref, wd_ref, gn_ref, (ho_ref,))


def _ffn_specs(g, ffn_w, g_next):
    wg, wu, wd = ffn_w
    return ([_resident((1, D_MODEL)), _resident(wg.shape), _resident(wu.shape), _resident(wd.shape),
             _resident((1, D_MODEL))], (g, wg, wu, wd, g_next))


def _mix_ffn(x, att, u, gv, ws, bs_rows, wo_bf16, g, ffn_w, g_next):
    m = x.shape[0]
    tile = lambda w: pl.BlockSpec((TOKEN_TILE, w), lambda i: (i, 0))
    ffn_specs, ffn_args = _ffn_specs(g, ffn_w, g_next)
    return pl.pallas_call(
        _mix_ffn_kernel,
        grid=(m // TOKEN_TILE,),
        in_specs=[tile(D_MODEL), tile(SB_WIDTH), tile(SGU_WIDTH), tile(SGU_WIDTH),
                  _resident(ws.shape), _resident(bs_rows.shape), _resident(wo_bf16.shape)] + ffn_specs,
        out_specs=[tile(D_MODEL)] * 2,
        out_shape=[jax.ShapeDtypeStruct((m, D_MODEL), F32)] * 2,
        scratch_shapes=[pltpu.VMEM((TOKEN_TILE, SGU_WIDTH), BF16)],
        compiler_params=_params("parallel"),
        name="mix_ffn",
    )(x, att, u, gv, ws, bs_rows, wo_bf16, *ffn_args)


def _glu_ffn(x, y, w_glu_bf16, g, ffn_w, g_next):
    m = x.shape[0]
    tile = pl.BlockSpec((TOKEN_TILE, D_MODEL), lambda i: (i, 0))
    ffn_specs, ffn_args = _ffn_specs(g, ffn_w, g_next)
    return pl.pallas_call(
        _glu_ffn_kernel,
        grid=(m // TOKEN_TILE,),
        in_specs=[tile, tile, _resident(w_glu_bf16.shape)] + ffn_specs,
        out_specs=tile,
        out_shape=jax.ShapeDtypeStruct((m, D_MODEL), F32),
        compiler_params=_params("parallel"),
        name="glu_ffn",
    )(x, y, w_glu_bf16, *ffn_args)


def _s5_tables_kernel(lr_ref, li_ref, ls_ref, bre_ref, bim_ref, cre_ref, cim_ref,
                      f_ref, et_ref, klag_ref, at_ref, f32_ref):
    half = S5_TILE_STATES
    lr, li = lr_ref[...], li_ref[...]
    step = jnp.exp(ls_ref[...])
    mag, ang = jnp.exp(lr * step), li * step
    ar, ai = mag * jnp.cos(ang), mag * jnp.sin(ang)
    den = lr * lr + li * li
    nr = ar - 1.0
    co_re = (nr * lr + ai * li) / den
    co_im = (ai * lr - nr * li) / den

    row_group = lax.broadcasted_iota(jnp.int32, (LANES, half), 0) // SSM_GROUP
    col_group = lax.broadcasted_iota(jnp.int32, (LANES, half), 1) // SSM_STATE

    def block_diag(ref):
        return jnp.where(row_group == col_group, jnp.concatenate([ref[...]] * S5_GROUPS_PER_TILE, axis=1), 0.0)

    b_re, b_im, c_re, c_im = (block_diag(r) for r in (bre_ref, bim_ref, cre_ref, cim_ref))
    g_re = co_re * b_re - co_im * b_im
    g_im = co_re * b_im + co_im * b_re
    p_re, p_im = ar, ai
    def put_halves(ref, block, re, im):
        for hs in range(2):
            lanes = slice(hs * S5_HALF_LANES, (hs + 1) * S5_HALF_LANES)
            states = slice(hs * S5_HALF_STATES, (hs + 1) * S5_HALF_STATES)
            rows = slice(block * S5_HALF_LANES, (block + 1) * S5_HALF_LANES)
            ref[hs, rows, :S5_HALF_STATES] = re[lanes, states].astype(BF16)
            ref[hs, rows, S5_HALF_STATES:] = im[lanes, states].astype(BF16)

    for tau in range(S5_T):
        rows = slice((S5_T - 1 - tau) * LANES, (S5_T - tau) * LANES)
        f32_ref[rows, :half] = g_re
        f32_ref[rows, half:] = g_im
        put_halves(f_ref, S5_T - 1 - tau, g_re, g_im)
        put_halves(et_ref, tau, c_re * p_re - c_im * p_im, -c_re * p_im - c_im * p_re)
        if tau < S5_T - 1:
            g_re, g_im = g_re * ar - g_im * ai, g_re * ai + g_im * ar
            p_re, p_im = p_re * ar - p_im * ai, p_re * ai + p_im * ar
    at_ref[0:1, :] = jnp.concatenate([p_re, p_re], axis=1)
    at_ref[1:2, :] = jnp.concatenate([-p_im, p_im], axis=1)

    f_all = f32_ref[...]
    f_hi = f_all.astype(BF16)
    f_lo = (f_all - f_hi.astype(F32)).astype(BF16)
    cc = jnp.concatenate([c_re, -c_im], axis=1)
    c_hi = cc.astype(BF16)
    c_lo = (cc - c_hi.astype(F32)).astype(BF16)
    k_all = _dot_nt(f_hi, c_hi) + _dot_nt(f_hi, c_lo) + _dot_nt(f_lo, c_hi)
    for tau in range(S5_T):
        rows = slice((S5_T - 1 - tau) * LANES, (S5_T - tau) * LANES)
        klag_ref[tau] = k_all[rows, :]


def _s5_tables(lam_re, lam_im, log_step, b_re, b_im, c_re, c_im):
    groups, nstate = lam_re.shape
    ntile = groups // S5_GROUPS_PER_TILE
    half = S5_TILE_STATES
    vec = lambda a: a.reshape(ntile, 1, half)
    mat = lambda a: a.reshape(ntile, LANES, nstate)
    args = (vec(lam_re), vec(lam_im), vec(jnp.repeat(log_step, nstate)),
            mat(jnp.swapaxes(b_re, 1, 2)), mat(jnp.swapaxes(b_im, 1, 2)), mat(c_re), mat(c_im))
    per_tile = lambda shape: pl.BlockSpec((None,) + shape, lambda k: (k,) + (0,) * len(shape))
    return pl.pallas_call(
        _s5_tables_kernel,
        grid=(ntile,),
        in_specs=[per_tile((1, half))] * 3 + [per_tile((LANES, nstate))] * 4,
        out_specs=[per_tile(S5_HALF_TABLE), per_tile(S5_HALF_TABLE),
                   per_tile((S5_T, LANES, LANES)), per_tile((2, 2 * half))],
        out_shape=[jax.ShapeDtypeStruct((ntile,) + S5_HALF_TABLE, BF16),
                   jax.ShapeDtypeStruct((ntile,) + S5_HALF_TABLE, BF16),
                   jax.ShapeDtypeStruct((ntile, S5_T, LANES, LANES), F32),
                   jax.ShapeDtypeStruct((ntile, 2, 2 * half), F32)],
        scratch_shapes=[pltpu.VMEM((S5_T * LANES, 2 * half), F32)],
        compiler_params=_params("parallel"),
        name="s5_tables",
    )(*args)


def _s5_kernel(h_ref, re0_ref, im0_ref, klag_ref, f_ref, et_ref, at_ref, d_ref,
               y_ref, reo_ref, imo_ref, wint_ref, xend_ref, sprev_ref, s_ref, pow1_ref, pow2_ref,
               *, nseq, seq_rows):
    first_of_tile = jnp.logical_and(pl.program_id(1) == 0, pl.program_id(2) == 0)
    nchunk = seq_rows // S5_T
    half = S5_TILE_STATES

    def halves(a, b):
        lane = lax.broadcasted_iota(jnp.int32, a.shape, 1)
        first = lane < S5_HALF_LANES
        return (jnp.where(first, a, pltpu.roll(b, S5_HALF_LANES, 1)),
                jnp.where(first, pltpu.roll(a, S5_HALF_LANES, 1), b))

    @pl.when(first_of_tile)
    def _():
        wint_ref[...] = jnp.zeros_like(wint_ref)
        zero = jnp.zeros((LANES, LANES), F32)
        for j in range(S5_T):
            for m in range(j // 2, S5_T // 2):
                lag = 2 * m - j
                low, high = halves(klag_ref[lag] if lag >= 0 else zero, klag_ref[lag + 1])
                rows = slice(j * S5_HALF_LANES, (j + 1) * S5_HALF_LANES)
                wint_ref[0, rows, m * LANES:(m + 1) * LANES] = low[:S5_HALF_LANES].astype(BF16)
                wint_ref[1, rows, m * LANES:(m + 1) * LANES] = high[S5_HALF_LANES:].astype(BF16)

    @pl.when(pl.program_id(2) == 0)
    def _():
        s_ref[:, :half] = re0_ref[...]
        s_ref[:, half:] = im0_ref[...]

    def slab(t):
        if nseq == 1:
            return [pl.ds(t, nchunk, stride=S5_T)]
        return [pl.ds(n * S5_T + t, nseq, stride=seq_rows) for n in range(nchunk)]

    u = []
    for t in range(S5_T):
        parts = [h_ref[rows, :] for rows in slab(t)]
        u.append(parts[0] if len(parts) == 1 else jnp.concatenate(parts, axis=0))
    split = [halves(u[2 * m], u[2 * m + 1]) for m in range(S5_T // 2)]
    uc = [jnp.concatenate([pair[hs].astype(BF16) for pair in split], axis=1) for hs in range(2)]

    nslab = 2 * half // LANES

    def load_rows(ref, rows):
        return jnp.concatenate([ref[q, rows, :] for q in range(nslab)], axis=1)

    def store_rows(ref, rows, val, add=False):
        for q in range(nslab):
            piece = val[:, q * LANES:(q + 1) * LANES]
            ref[q, rows, :] = ref[q, rows, :] + piece if add else piece

    hstates = S5_HALF_STATES
    hslabs = hstates // LANES
    for hs in range(2):
        x = _dot(uc[hs], f_ref[hs])
        for part in range(2):
            for q in range(hslabs):
                col = part * hstates + q * LANES
                xend_ref[part * (nslab // 2) + hs * hslabs + q] = x[:, col:col + LANES]

    a1 = at_ref[0:1, :]
    a2 = at_ref[1:2, :]
    swap = lambda s: jnp.concatenate([s[:, half:], s[:, :half]], axis=1)
    cmul = lambda p1, p2, s: p1 * s + p2 * swap(s)

    if nseq > 1:
        def step(n, s):
            rows = pl.ds(n * nseq, nseq)
            store_rows(sprev_ref, rows, s)
            return cmul(a1, a2, s) + load_rows(xend_ref, rows)

        s_fin = lax.fori_loop(0, nchunk, step, s_ref[...], unroll=min(S5_SCAN_UNROLL, nchunk))
    else:
        seg_len = nchunk // S5_SEGMENTS

        @pl.when(first_of_tile)
        def _():
            z = jnp.concatenate([jnp.ones((1, half), F32), jnp.zeros((1, half), F32)], axis=1)
            for i in range(seg_len + 1):
                pow1_ref[i:i + 1, :] = jnp.concatenate([z[:, :half], z[:, :half]], axis=1)
                pow2_ref[i:i + 1, :] = jnp.concatenate([-z[:, half:], z[:, half:]], axis=1)
                z = cmul(a1, a2, z)

        def local_step(i, s):
            rows = pl.ds(i, S5_SEGMENTS, stride=seg_len)
            store_rows(sprev_ref, rows, s)
            return cmul(a1, a2, s) + load_rows(xend_ref, rows)

        local_end = lax.fori_loop(0, seg_len, local_step, jnp.zeros((S5_SEGMENTS, 2 * half), F32),
                                  unroll=S5_SCAN_UNROLL)
        seg1 = pow1_ref[seg_len:seg_len + 1, :]
        seg2 = pow2_ref[seg_len:seg_len + 1, :]
        start = s_ref[...]
        for seg in range(S5_SEGMENTS):
            rows = slice(seg * seg_len, (seg + 1) * seg_len)
            store_rows(sprev_ref, rows, pow1_ref[:seg_len, :] * start + pow2_ref[:seg_len, :] * swap(start),
                       add=True)
            start = cmul(seg1, seg2, start) + local_end[seg:seg + 1, :]
        s_fin = start
    s_ref[...] = s_fin

    @pl.when(pl.program_id(2) == pl.num_programs(2) - 1)
    def _():
        reo_ref[...] = s_fin[:, :half]
        imo_ref[...] = s_fin[:, half:]

    sprev = [jnp.concatenate([sprev_ref[part * (nslab // 2) + hs * hslabs + q]
                              for part in range(2) for q in range(hslabs)], axis=1).astype(BF16)
             for hs in range(2)]
    d = d_ref[...]
    pair = 2 * LANES
    for cp in range(S5_T // 4):
        depth = (4 * cp + 4) * S5_HALF_LANES
        cols = slice(cp * pair, (cp + 1) * pair)
        y2 = [_dot(uc[hs][:, :depth], wint_ref[hs, :depth, cols]) + _dot_nt(sprev[hs], et_ref[hs, cols, :])
              for hs in range(2)]
        for i in range(2):
            m = 2 * cp + i
            tokens = halves(y2[0][:, i * LANES:(i + 1) * LANES], y2[1][:, i * LANES:(i + 1) * LANES])
            for t, y in zip((2 * m, 2 * m + 1), tokens):
                yt = _gelu(y + d * u[t])
                off = 0
                for rows in slab(t):
                    y_ref[rows, :] = yt[off:off + rows.size, :]
                    off += rows.size


def _s5(h, re0, im0, klag, f_tab, et_tab, a_t, d_skip, nseq, seq_rows, blocks_per_seq):
    tokens = h.shape[0]
    nb = re0.shape[0]
    ntile = D_MODEL // LANES
    half = S5_TILE_STATES
    rows = nseq * seq_rows
    nrow = rows // S5_T
    npow = -(-(nrow // S5_SEGMENTS + 1) // 8) * 8
    hspec = pl.BlockSpec((rows, LANES), lambda k, b, j: (b * blocks_per_seq + j, k))
    sspec = pl.BlockSpec((None, nseq, half), lambda k, b, j: (b, 0, k))
    per_tile = lambda shape: pl.BlockSpec((None,) + shape, lambda k, b, j: (k,) + (0,) * len(shape))
    kern = functools.partial(_s5_kernel, nseq=nseq, seq_rows=seq_rows)
    return pl.pallas_call(
        kern,
        grid=(ntile, nb, blocks_per_seq),
        in_specs=[hspec, sspec, sspec,
                  per_tile((S5_T, LANES, LANES)), per_tile(S5_HALF_TABLE), per_tile(S5_HALF_TABLE),
                  per_tile((2, 2 * half)),
                  pl.BlockSpec((1, LANES), lambda k, b, j: (0, k))],
        out_specs=[hspec, sspec, sspec],
        out_shape=[jax.ShapeDtypeStruct((tokens, D_MODEL), F32),
                   jax.ShapeDtypeStruct(re0.shape, F32), jax.ShapeDtypeStruct(im0.shape, F32)],
        scratch_shapes=[pltpu.VMEM((2, S5_T * S5_HALF_LANES, S5_T * S5_HALF_LANES), BF16),
                        pltpu.VMEM((2 * half // LANES, nrow, LANES), F32),
                        pltpu.VMEM((2 * half // LANES, nrow, LANES), F32),
                        pltpu.VMEM((nseq, 2 * half), F32),
                        pltpu.VMEM((npow, 2 * half), F32),
                        pltpu.VMEM((npow, 2 * half), F32)],
        compiler_params=_params("arbitrary", "arbitrary", "arbitrary"),
        name="s5",
    )(h, re0, im0, klag, f_tab, et_tab, a_t, d_skip)


def kernel(x_prompt, x_sample, cache_sb_k, cache_sb_v, state_ssm_re, state_ssm_im, norm_mix, norm_ffn,
           norm_final, ab_w_in, sgu_w, sgu_b, ab_w_out, ssm_lam_re, ssm_lam_im, ssm_log_step, ssm_b_re,
           ssm_b_im, ssm_c_re, ssm_c_im, ssm_d, ssm_w_glu, ffn_w_gate, ffn_w_up, ffn_w_down):
    bsz, seq, _ = x_prompt.shape
    dbsz, dseq, _ = x_sample.shape
    past = cache_sb_k.shape[2]
    heads = SB_WIDTH // HEAD_DIM
    row = lambda v: v.reshape(1, -1)

    xp = x_prompt.reshape(bsz * seq, D_MODEL)
    xs = x_sample.reshape(dbsz * dseq, D_MODEL)

    w_in = ab_w_in[0].astype(BF16)
    w_out = ab_w_out[0].astype(BF16)
    bs_rows = jnp.repeat(sgu_b[0].T, SGU_WIDTH // SGU_GROUPS, axis=1)
    g_mix0 = row(norm_mix[0])

    qp, kp, vp, kpb, vpb, up, gp = _proj(xp, g_mix0, w_in)
    qs, ks, vs, ksb, vsb, us, gs = _proj(xs, g_mix0, w_in)
    att_p = _sb_prompt(qp, kpb, vpb, bsz, seq)
    to_hdp = lambda c: jnp.transpose(c[0], (0, 2, 3, 1))
    att_s = _sb_sample(qs, ksb, vsb, to_hdp(cache_sb_k), to_hdp(cache_sb_v), dbsz, dseq)
    ffn_w = [(ffn_w_gate[l].astype(BF16), ffn_w_up[l].astype(BF16), ffn_w_down[l].astype(BF16)) for l in range(2)]
    g_mix1 = row(norm_mix[1])
    xp, hp = _mix_ffn(xp, att_p, up, gp, sgu_w[0], bs_rows, w_out, row(norm_ffn[0]), ffn_w[0], g_mix1)
    xs, hs = _mix_ffn(xs, att_s, us, gs, sgu_w[0][:, :dseq, :dseq], bs_rows[:dseq], w_out,
                      row(norm_ffn[0]), ffn_w[0], g_mix1)

    f_tab, et_tab, klag, a_t = _s5_tables(ssm_lam_re[0], ssm_lam_im[0], ssm_log_step[0], ssm_b_re[0],
                                          ssm_b_im[0], ssm_c_re[0], ssm_c_im[0])
    d_skip = row(ssm_d[0])
    nstates = state_ssm_re.shape[2] * state_ssm_re.shape[3]
    zeros = jnp.zeros((bsz, 1, nstates), F32)
    prompt_block = 8192
    yp, rp, ip = _s5(hp, zeros, zeros, klag, f_tab, et_tab, a_t, d_skip, 1, prompt_block, seq // prompt_block)
    ys, rs, is_ = _s5(hs, state_ssm_re[0].reshape(1, dbsz, nstates), state_ssm_im[0].reshape(1, dbsz, nstates),
                      klag, f_tab, et_tab, a_t, d_skip, dbsz, dseq, 1)
    w_glu = ssm_w_glu[0].astype(BF16)
    y_prompt = _glu_ffn(xp, yp, w_glu, row(norm_ffn[1]), ffn_w[1], row(norm_final))
    y_sample = _glu_ffn(xs, ys, w_glu, row(norm_ffn[1]), ffn_w[1], row(norm_final))

    state_shape = state_ssm_re.shape[2:]
    return (y_prompt.reshape(bsz, seq, D_MODEL), y_sample.reshape(dbsz, dseq, D_MODEL),
            kp.reshape(1, bsz, seq, heads, HEAD_DIM), vp.reshape(1, bsz, seq, heads, HEAD_DIM),
            ks.reshape(1, dbsz, dseq, heads, HEAD_DIM), vs.reshape(1, dbsz, dseq, heads, HEAD_DIM),
            gs.reshape(1, dbsz, dseq, SGU_WIDTH),
            rp.reshape((1, bsz) + state_shape), ip.reshape((1, bsz) + state_shape),
            rs.reshape((1, dbsz) + state_shape), is_.reshape((1, dbsz) + state_shape))
```

```python
import functools
import math

import jax
import jax.numpy as jnp
from jax import lax
from jax.experimental import pallas as pl
from jax.experimental.pallas import tpu as pltpu

F32 = jnp.float32
BF16 = jnp.bfloat16

LANES = 128
VMEM_LIMIT = 56 * 1024 * 1024

D_MODEL = 1024
HEAD_DIM = 64
SB_WIDTH = 512
SGU_WIDTH = 512
SGU_GROUPS = 8
SGU_CHUNK = 128
SSM_GROUP = 16
SSM_STATE = 64
FFN_HIDDEN = 2816
RMS_EPS = 1e-6

TOKEN_TILE = 512
SB_BLOCK = 128
SB_HEADS = SB_WIDTH // HEAD_DIM
SB_PAIRS = SB_WIDTH // LANES
SB_CUTOFF = -104.0
SB_PROMPT_FUSED_BLOCKS = 3
S5_T = 16
S5_SCAN_UNROLL = 8
S5_SEGMENTS = 8
S5_GROUPS_PER_TILE = LANES // SSM_GROUP
S5_TILE_STATES = S5_GROUPS_PER_TILE * SSM_STATE
S5_HALF_LANES = LANES // 2
S5_HALF_STATES = S5_TILE_STATES // 2
S5_HALF_TABLE = (2, S5_T * S5_HALF_LANES, 2 * S5_HALF_STATES)


def _rmsnorm(x, g):
    return x * lax.rsqrt(jnp.mean(x * x, axis=-1, keepdims=True) + RMS_EPS) * g


def _gelu(x):
    return 0.5 * x * (1.0 + jnp.tanh(math.sqrt(2.0 / math.pi) * (x + 0.044715 * (x * x * x))))


def _sigmoid(x):
    return 1.0 / (1.0 + jnp.exp(-x))


def _dot(a, b):
    return jnp.dot(a, b, preferred_element_type=F32)


def _dot_nt(a, b):
    return lax.dot_general(a, b, (((1,), (1,)), ((), ())), preferred_element_type=F32)


def _params(*sem):
    return pltpu.CompilerParams(dimension_semantics=sem, vmem_limit_bytes=VMEM_LIMIT)


def _cast_kernel(x_ref, o_ref):
    o_ref[...] = x_ref[...].astype(o_ref.dtype)


def _to_bf16(w):
    layers, rows, cols = w.shape
    block = math.gcd(rows, TOKEN_TILE)
    spec = pl.BlockSpec((None, block, cols), lambda l, i: (l, i, 0))
    return pl.pallas_call(
        _cast_kernel,
        grid=(layers, rows // block),
        in_specs=[spec],
        out_specs=spec,
        out_shape=jax.ShapeDtypeStruct(w.shape, BF16),
        compiler_params=_params("parallel", "parallel"),
        name="to_bf16",
    )(w)


def _resident(shape):
    nd = len(shape)
    return pl.BlockSpec(shape, lambda *_: (0,) * nd, pipeline_mode=pl.Buffered(1))


def _proj_kernel(x_ref, g_ref, w_ref, q_ref, k_ref, v_ref, kb_ref, vb_ref, u_ref, gv_ref):
    h = _rmsnorm(x_ref[...], g_ref[...]).astype(BF16)
    col = lambda i: _dot(h, w_ref[:, i * SB_WIDTH:(i + 1) * SB_WIDTH])
    for f32_ref, bf16_ref, z in ((k_ref, kb_ref, col(1)), (v_ref, vb_ref, col(2))):
        bf16_ref[...] = z.astype(BF16)
        f32_ref[...] = z.reshape(f32_ref.shape)
    q_ref[...] = (col(0) * (HEAD_DIM ** -0.5)).astype(BF16)
    u_ref[...] = _gelu(col(3))
    gv_ref[...] = _gelu(col(4))


def _proj(x, g, w_bf16):
    m = x.shape[0]
    tile = lambda w: pl.BlockSpec((TOKEN_TILE, w), lambda i: (i, 0))
    heads = pl.BlockSpec((TOKEN_TILE, SB_HEADS, HEAD_DIM), lambda i: (i, 0, 0))
    flat = lambda dt: jax.ShapeDtypeStruct((m, SB_WIDTH), dt)
    by_head = jax.ShapeDtypeStruct((m, SB_HEADS, HEAD_DIM), F32)
    return pl.pallas_call(
        _proj_kernel,
        grid=(m // TOKEN_TILE,),
        in_specs=[tile(D_MODEL), _resident((1, D_MODEL)), _resident(w_bf16.shape)],
        out_specs=[tile(SB_WIDTH), heads, heads] + [tile(SB_WIDTH)] * 4,
        out_shape=[flat(BF16), by_head, by_head, flat(BF16), flat(BF16), flat(F32), flat(F32)],
        compiler_params=_params("parallel"),
        name="proj",
    )(x, g, w_bf16)


def _suffix_and_total():
    j = lax.broadcasted_iota(jnp.int32, (SB_BLOCK, 2 * SB_BLOCK), 0)
    s = lax.broadcasted_iota(jnp.int32, (SB_BLOCK, 2 * SB_BLOCK), 1)
    return jnp.where(jnp.logical_or(j > s, s >= SB_BLOCK), 1.0, 0.0).astype(BF16)


def _sb_weights(z, c, sums_mat, masks):
    rows = z.shape[0]
    nb = len(masks)
    log_beta = jnp.minimum(z, 0.0) - jnp.log(1.0 + jnp.exp(-jnp.abs(z)))
    log_stay = log_beta - z
    parts = []
    for j, mask in enumerate(masks):
        stay = log_stay[:, j * SB_BLOCK:(j + 1) * SB_BLOCK]
        if mask is not None:
            stay = jnp.where(mask, stay, 0.0)
        parts.append(stay.astype(BF16))
    sums = _dot(parts[0] if nb == 1 else jnp.concatenate(parts, axis=0), sums_mat)
    ws = [None] * nb
    for j in reversed(range(nb)):
        s = sums[j * rows:(j + 1) * rows]
        w = jnp.exp(log_beta[:, j * SB_BLOCK:(j + 1) * SB_BLOCK] + s[:, :SB_BLOCK] + c)
        if masks[j] is not None:
            w = jnp.where(masks[j], w, 0.0)
        ws[j] = w.astype(BF16)
        c = c + s[:, SB_BLOCK:]
    return (ws[0] if nb == 1 else jnp.concatenate(ws, axis=1)), c


def _sb_alive(c):
    return jnp.max(c) > SB_CUTOFF


def _sb_pair_block(q_ref, o_ref, sums_mat):
    nq = q_ref.shape[0]
    lane = lax.broadcasted_iota(jnp.int32, (nq, LANES), 1)
    q_pairs = []
    for p in range(SB_PAIRS):
        q = q_ref[:, p * LANES:(p + 1) * LANES]
        zero = jnp.zeros_like(q)
        q_pairs.append(jnp.concatenate([jnp.where(lane < HEAD_DIM, q, zero),
                                        jnp.where(lane < HEAD_DIM, zero, q)], axis=0))

    def block(kblk, vblk, c, masks, first, transposed=False):
        qk = _dot if transposed else _dot_nt
        pv_dot = _dot_nt if transposed else _dot
        z = jnp.concatenate([qk(q_pairs[p], kblk[p]) for p in range(SB_PAIRS)], axis=0)
        w, c = _sb_weights(z, c, sums_mat, masks)
        for p in range(SB_PAIRS):
            cols = slice(p * LANES, (p + 1) * LANES)
            pv = pv_dot(w[2 * p * nq:(2 * p + 2) * nq], vblk[p])
            merged = jnp.where(lane < HEAD_DIM, pv[:nq], pv[nq:])
            o_ref[:, cols] = merged if first else o_ref[:, cols] + merged
        return c

    return block


def _pair_cols(x):
    return [x[:, p * LANES:(p + 1) * LANES] for p in range(SB_PAIRS)]


def _sb_causal(nq):
    row = lax.broadcasted_iota(jnp.int32, (SB_HEADS * nq, SB_BLOCK), 0)
    col = lax.broadcasted_iota(jnp.int32, (SB_HEADS * nq, SB_BLOCK), 1)
    return col < (row & (nq - 1))


def _sb_prompt_kernel(q_ref, k_ref, v_ref, o_ref):
    qb = pl.program_id(1)
    block = _sb_pair_block(q_ref, o_ref, _suffix_and_total())
    causal = _sb_causal(SB_BLOCK)
    c_init = jnp.zeros((SB_HEADS * SB_BLOCK, SB_BLOCK), F32)

    def newest(nb):
        rows = pl.ds(pl.multiple_of((qb - (nb - 1)) * SB_BLOCK, SB_BLOCK), nb * SB_BLOCK)
        return block(_pair_cols(k_ref[rows, :]), _pair_cols(v_ref[rows, :]), c_init,
                     [None] * (nb - 1) + [causal], True)

    fused = SB_PROMPT_FUSED_BLOCKS
    c = lax.cond(qb >= fused - 1, lambda: newest(fused), lambda: newest(1))

    def cond(state):
        kb, c = state
        return jnp.logical_and(kb >= 0, _sb_alive(c))

    def body(state):
        kb, c = state
        rows = pl.ds(pl.multiple_of(kb * SB_BLOCK, SB_BLOCK), SB_BLOCK)
        return kb - 1, block(_pair_cols(k_ref[rows, :]), _pair_cols(v_ref[rows, :]), c, [None], False)

    lax.while_loop(cond, body, (jnp.where(qb >= fused - 1, qb - fused, qb - 1), c))


def _sb_prompt(q, k, v, bsz, seq):
    nq = seq // SB_BLOCK
    qspec = pl.BlockSpec((SB_BLOCK, SB_WIDTH), lambda b, i: (b * nq + i, 0))
    kvspec = pl.BlockSpec((seq, SB_WIDTH), lambda b, i: (b, 0), pipeline_mode=pl.Buffered(1))
    return pl.pallas_call(
        _sb_prompt_kernel,
        grid=(bsz, nq),
        in_specs=[qspec, kvspec, kvspec],
        out_specs=qspec,
        out_shape=jax.ShapeDtypeStruct((bsz * seq, SB_WIDTH), F32),
        compiler_params=_params("parallel", "arbitrary"),
        name="sb_prompt",
    )(q, k, v)


def _sb_sample_kernel(q_ref, k_ref, v_ref, ck_hbm, cv_hbm, o_ref, kbuf, vbuf, sem):
    b = pl.program_id(0)
    nq = q_ref.shape[0]
    last = ck_hbm.shape[3] // SB_BLOCK - 1

    def fetch(kb, slot):
        pos = pl.ds(pl.multiple_of(kb * SB_BLOCK, SB_BLOCK), SB_BLOCK)
        return (pltpu.make_async_copy(ck_hbm.at[b, :, :, pos], kbuf.at[slot], sem.at[0, slot]),
                pltpu.make_async_copy(cv_hbm.at[b, :, :, pos], vbuf.at[slot], sem.at[1, slot]))

    def pair_rows(buf, slot):
        return [buf[slot, 2 * p:2 * p + 2].reshape(2 * HEAD_DIM, SB_BLOCK).astype(BF16) for p in range(SB_PAIRS)]

    for cp in fetch(last, last % 2):
        cp.start()

    sums_mat = _suffix_and_total()
    pad = jnp.zeros((SB_BLOCK - k_ref.shape[0], SB_WIDTH), BF16)
    new_k = jnp.concatenate([k_ref[...], pad], axis=0)
    new_v = jnp.concatenate([v_ref[...], pad], axis=0)
    block = _sb_pair_block(q_ref, o_ref, sums_mat)
    c = block(_pair_cols(new_k), _pair_cols(new_v), jnp.zeros((SB_HEADS * nq, SB_BLOCK), F32),
              [_sb_causal(nq)], True)

    def cond(state):
        kb, c = state
        return jnp.logical_and(kb >= 0, _sb_alive(c))

    def body(state):
        kb, c = state
        slot = kb % 2
        for cp in fetch(kb, slot):
            cp.wait()

        @pl.when(kb >= 1)
        def _():
            for cp in fetch(kb - 1, 1 - slot):
                cp.start()

        return kb - 1, block(pair_rows(kbuf, slot), pair_rows(vbuf, slot), c, [None], False, transposed=True)

    kb_end, _ = lax.while_loop(cond, body, (jnp.int32(last), c))

    @pl.when(kb_end >= 0)
    def _():
        for cp in fetch(kb_end, kb_end % 2):
            cp.wait()


def _sb_sample(q, k, v, cache_k, cache_v, bsz, n):
    new = pl.BlockSpec((n, SB_WIDTH), lambda b: (b, 0))
    old = pl.BlockSpec(memory_space=pl.ANY)
    slots = pltpu.VMEM((2, SB_HEADS, HEAD_DIM, SB_BLOCK), F32)
    return pl.pallas_call(
        _sb_sample_kernel,
        grid=(bsz,),
        in_specs=[new, new, new, old, old],
        out_specs=new,
        out_shape=jax.ShapeDtypeStruct((bsz * n, SB_WIDTH), F32),
        scratch_shapes=[slots, slots, pltpu.SemaphoreType.DMA((2, 2))],
        compiler_params=_params("arbitrary"),
        name="sb_sample",
    )(q, k, v, cache_k, cache_v)


FFN_CHUNK = 256


def _ffn_tail(x, g_ref, wg_ref, wu_ref, wd_ref, gn_ref, o_refs):
    h = _rmsnorm(x, g_ref[...]).astype(BF16)
    acc = x
    for c in range(FFN_HIDDEN // FFN_CHUNK):
        cols = slice(c * FFN_CHUNK, (c + 1) * FFN_CHUNK)
        gate = _dot(h, wg_ref[:, cols])
        up = _dot(h, wu_ref[:, cols])
        act = (gate * _sigmoid(gate) * up).astype(BF16)
        acc = acc + _dot(act, wd_ref[cols, :])
    if len(o_refs) == 2:
        o_refs[0][...] = acc
    o_refs[-1][...] = _rmsnorm(acc, gn_ref[...])


def _mix_ffn_kernel(x_ref, att_ref, u_ref, gv_ref, ws_ref, bs_ref, wo_ref,
                    g_ref, wg_ref, wu_ref, wd_ref, gn_ref, xo_ref, ho_ref, sg_ref):
    chunk = ws_ref.shape[1]
    t = lax.broadcasted_iota(jnp.int32, (chunk, chunk), 0)
    s = lax.broadcasted_iota(jnp.int32, (chunk, chunk), 1)
    tri = s <= t
    ws = [jnp.where(tri, ws_ref[g], 0.0).astype(BF16) for g in range(SGU_GROUPS)]
    lane = lax.broadcasted_iota(jnp.int32, (chunk, LANES), 1)
    group_dim = SGU_WIDTH // SGU_GROUPS
    for c in range(TOKEN_TILE // chunk):
        rows = slice(c * chunk, (c + 1) * chunk)
        for p in range(SGU_WIDTH // LANES):
            cols = slice(p * LANES, (p + 1) * LANES)
            gv = gv_ref[rows, cols].astype(BF16)
            mixed = jnp.where(lane < group_dim, _dot(ws[2 * p], gv), _dot(ws[2 * p + 1], gv)) + bs_ref[:, cols]
            sg_ref[rows, cols] = (u_ref[rows, cols] * mixed).astype(BF16)
    xo_ref[...] = (x_ref[...] + _dot(att_ref[...].astype(BF16), wo_ref[:SB_WIDTH, :])
                   + _dot(sg_ref[...], wo_ref[SB_WIDTH:, :]))
    _ffn_tail(xo_ref[...], g_ref, wg_ref, wu_ref, wd_ref, gn_ref, (xo_ref, ho_ref))


def _glu_ffn_kernel(x_ref, y_ref, w_ref, g_ref, wg_ref, wu_ref, wd_ref, gn_ref, ho_ref):
    y = y_ref[...].astype(BF16)
    for c in range(D_MODEL // FFN_CHUNK):
        cols = slice(c * FFN_CHUNK, (c + 1) * FFN_CHUNK)
        ga = _dot(y, w_ref[:, cols])
        gb = _dot(y, w_ref[:, D_MODEL + c * FFN_CHUNK:D_MODEL + (c + 1) * FFN_CHUNK])
        ho_ref[:, cols] = x_ref[:, cols] + ga * _sigmoid(gb)
    _ffn_tail(ho_ref[...], g_ref, wg_ref, wu_ref, wd_ref, gn_ref, (ho_ref,))


def _ffn_specs(g, ffn_w, g_next):
    wg, wu, wd = ffn_w
    return ([_resident((1, D_MODEL)), _resident(wg.shape), _resident(wu.shape), _resident(wd.shape),
             _resident((1, D_MODEL))], (g, wg, wu, wd, g_next))


def _mix_ffn(x, att, u, gv, ws, bs_rows, wo_bf16, g, ffn_w, g_next):
    m = x.shape[0]
    tile = lambda w: pl.BlockSpec((TOKEN_TILE, w), lambda i: (i, 0))
    ffn_specs, ffn_args = _ffn_specs(g, ffn_w, g_next)
    return pl.pallas_call(
        _mix_ffn_kernel,
        grid=(m // TOKEN_TILE,),
        in_specs=[tile(D_MODEL), tile(SB_WIDTH), tile(SGU_WIDTH), tile(SGU_WIDTH),
                  _resident(ws.shape), _resident(bs_rows.shape), _resident(wo_bf16.shape)] + ffn_specs,
        out_specs=[tile(D_MODEL)] * 2,
        out_shape=[jax.ShapeDtypeStruct((m, D_MODEL), F32)] * 2,
        scratch_shapes=[pltpu.VMEM((TOKEN_TILE, SGU_WIDTH), BF16)],
        compiler_params=_params("parallel"),
        name="mix_ffn",
    )(x, att, u, gv, ws, bs_rows, wo_bf16, *ffn_args)


def _glu_ffn(x, y, w_glu_bf16, g, ffn_w, g_next):
    m = x.shape[0]
    tile = pl.BlockSpec((TOKEN_TILE, D_MODEL), lambda i: (i, 0))
    ffn_specs, ffn_args = _ffn_specs(g, ffn_w, g_next)
    return pl.pallas_call(
        _glu_ffn_kernel,
        grid=(m // TOKEN_TILE,),
        in_specs=[tile, tile, _resident(w_glu_bf16.shape)] + ffn_specs,
        out_specs=tile,
        out_shape=jax.ShapeDtypeStruct((m, D_MODEL), F32),
        compiler_params=_params("parallel"),
        name="glu_ffn",
    )(x, y, w_glu_bf16, *ffn_args)


def _s5_tables_kernel(lr_ref, li_ref, ls_ref, bre_ref, bim_ref, cre_ref, cim_ref,
                      f_ref, et_ref, klag_ref, at_ref, f32_ref):
    half = S5_TILE_STATES
    lr, li = lr_ref[...], li_ref[...]
    step = jnp.exp(ls_ref[...])
    mag, ang = jnp.exp(lr * step), li * step
    ar, ai = mag * jnp.cos(ang), mag * jnp.sin(ang)
    den = lr * lr + li * li
    nr = ar - 1.0
    co_re = (nr * lr + ai * li) / den
    co_im = (ai * lr - nr * li) / den

    row_group = lax.broadcasted_iota(jnp.int32, (LANES, half), 0) // SSM_GROUP
    col_group = lax.broadcasted_iota(jnp.int32, (LANES, half), 1) // SSM_STATE

    def block_diag(ref):
        return jnp.where(row_group == col_group, jnp.concatenate([ref[...]] * S5_GROUPS_PER_TILE, axis=1), 0.0)

    b_re, b_im, c_re, c_im = (block_diag(r) for r in (bre_ref, bim_ref, cre_ref, cim_ref))
    g_re = co_re * b_re - co_im * b_im
    g_im = co_re * b_im + co_im * b_re
    p_re, p_im = ar, ai
    def put_halves(ref, block, re, im):
        for hs in range(2):
            lanes = slice(hs * S5_HALF_LANES, (hs + 1) * S5_HALF_LANES)
            states = slice(hs * S5_HALF_STATES, (hs + 1) * S5_HALF_STATES)
            rows = slice(block * S5_HALF_LANES, (block + 1) * S5_HALF_LANES)
            ref[hs, rows, :S5_HALF_STATES] = re[lanes, states].astype(BF16)
            ref[hs, rows, S5_HALF_STATES:] = im[lanes, states].astype(BF16)

    for tau in range(S5_T):
        rows = slice((S5_T - 1 - tau) * LANES, (S5_T - tau) * LANES)
        f32_ref[rows, :half] = g_re
        f32_ref[rows, half:] = g_im
        put_halves(f_ref, S5_T - 1 - tau, g_re, g_im)
        put_halves(et_ref, tau, c_re * p_re - c_im * p_im, -c_re * p_im - c_im * p_re)
        if tau < S5_T - 1:
            g_re, g_im = g_re * ar - g_im * ai, g_re * ai + g_im * ar
            p_re, p_im = p_re * ar - p_im * ai, p_re * ai + p_im * ar
    at_ref[0:1, :] = jnp.concatenate([p_re, p_re], axis=1)
    at_ref[1:2, :] = jnp.concatenate([-p_im, p_im], axis=1)

    f_all = f32_ref[...]
    f_hi = f_all.astype(BF16)
    f_lo = (f_all - f_hi.astype(F32)).astype(BF16)
    cc = jnp.concatenate([c_re, -c_im], axis=1)
    c_hi = cc.astype(BF16)
    c_lo = (cc - c_hi.astype(F32)).astype(BF16)
    k_all = _dot_nt(f_hi, c_hi) + _dot_nt(f_hi, c_lo) + _dot_nt(f_lo, c_hi)
    for tau in range(S5_T):
        rows = slice((S5_T - 1 - tau) * LANES, (S5_T - tau) * LANES)
        klag_ref[tau] = k_all[rows, :]


def _s5_tables(lam_re, lam_im, log_step, b_re, b_im, c_re, c_im):
    groups, nstate = lam_re.shape
    ntile = groups // S5_GROUPS_PER_TILE
    half = S5_TILE_STATES
    vec = lambda a: a.reshape(ntile, 1, half)
    mat = lambda a: a.reshape(ntile, LANES, nstate)
    args = (vec(lam_re), vec(lam_im), vec(jnp.repeat(log_step, nstate)),
            mat(jnp.swapaxes(b_re, 1, 2)), mat(jnp.swapaxes(b_im, 1, 2)), mat(c_re), mat(c_im))
    per_tile = lambda shape: pl.BlockSpec((None,) + shape, lambda k: (k,) + (0,) * len(shape))
    return pl.pallas_call(
        _s5_tables_kernel,
        grid=(ntile,),
        in_specs=[per_tile((1, half))] * 3 + [per_tile((LANES, nstate))] * 4,
        out_specs=[per_tile(S5_HALF_TABLE), per_tile(S5_HALF_TABLE),
                   per_tile((S5_T, LANES, LANES)), per_tile((2, 2 * half))],
        out_shape=[jax.ShapeDtypeStruct((ntile,) + S5_HALF_TABLE, BF16),
                   jax.ShapeDtypeStruct((ntile,) + S5_HALF_TABLE, BF16),
                   jax.ShapeDtypeStruct((ntile, S5_T, LANES, LANES), F32),
                   jax.ShapeDtypeStruct((ntile, 2, 2 * half), F32)],
        scratch_shapes=[pltpu.VMEM((S5_T * LANES, 2 * half), F32)],
        compiler_params=_params("parallel"),
        name="s5_tables",
    )(*args)


def _s5_kernel(h_ref, re0_ref, im0_ref, klag_ref, f_ref, et_ref, at_ref, d_ref,
               y_ref, reo_ref, imo_ref, wint_ref, xend_ref, sprev_ref, s_ref, pow1_ref, pow2_ref,
               *, nseq, seq_rows):
    first_of_tile = jnp.logical_and(pl.program_id(1) == 0, pl.program_id(2) == 0)
    nchunk = seq_rows // S5_T
    half = S5_TILE_STATES

    def halves(a, b):
        lane = lax.broadcasted_iota(jnp.int32, a.shape, 1)
        first = lane < S5_HALF_LANES
        return (jnp.where(first, a, pltpu.roll(b, S5_HALF_LANES, 1)),
                jnp.where(first, pltpu.roll(a, S5_HALF_LANES, 1), b))

    @pl.when(first_of_tile)
    def _():
        wint_ref[...] = jnp.zeros_like(wint_ref)
        zero = jnp.zeros((LANES, LANES), F32)
        for j in range(S5_T):
            for m in range(j // 2, S5_T // 2):
                lag = 2 * m - j
                low, high = halves(klag_ref[lag] if lag >= 0 else zero, klag_ref[lag + 1])
                rows = slice(j * S5_HALF_LANES, (j + 1) * S5_HALF_LANES)
                wint_ref[0, rows, m * LANES:(m + 1) * LANES] = low[:S5_HALF_LANES].astype(BF16)
                wint_ref[1, rows, m * LANES:(m + 1) * LANES] = high[S5_HALF_LANES:].astype(BF16)

    @pl.when(pl.program_id(2) == 0)
    def _():
        s_ref[:, :half] = re0_ref[...]
        s_ref[:, half:] = im0_ref[...]

    def slab(t):
        if nseq == 1:
            return [pl.ds(t, nchunk, stride=S5_T)]
        return [pl.ds(n * S5_T + t, nseq, stride=seq_rows) for n in range(nchunk)]

    u = []
    for t in range(S5_T):
        parts = [h_ref[rows, :] for rows in slab(t)]
        u.append(parts[0] if len(parts) == 1 else jnp.concatenate(parts, axis=0))
    split = [halves(u[2 * m], u[2 * m + 1]) for m in range(S5_T // 2)]
    uc = [jnp.concatenate([pair[hs].astype(BF16) for pair in split], axis=1) for hs in range(2)]

    nslab = 2 * half // LANES

    def load_rows(ref, rows):
        return jnp.concatenate([ref[q, rows, :] for q in range(nslab)], axis=1)

    def store_rows(ref, rows, val, add=False):
        for q in range(nslab):
            piece = val[:, q * LANES:(q + 1) * LANES]
            ref[q, rows, :] = ref[q, rows, :] + piece if add else piece

    hstates = S5_HALF_STATES
    hslabs = hstates // LANES
    for hs in range(2):
        x = _dot(uc[hs], f_ref[hs])
        for part in range(2):
            for q in range(hslabs):
                col = part * hstates + q * LANES
                xend_ref[part * (nslab // 2) + hs * hslabs + q] = x[:, col:col + LANES]

    a1 = at_ref[0:1, :]
    a2 = at_ref[1:2, :]
    swap = lambda s: jnp.concatenate([s[:, half:], s[:, :half]], axis=1)
    cmul = lambda p1, p2, s: p1 * s + p2 * swap(s)

    if nseq > 1:
        def step(n, s):
            rows = pl.ds(n * nseq, nseq)
            store_rows(sprev_ref, rows, s)
            return cmul(a1, a2, s) + load_rows(xend_ref, rows)

        s_fin = lax.fori_loop(0, nchunk, step, s_ref[...], unroll=min(S5_SCAN_UNROLL, nchunk))
    else:
        seg_len = nchunk // S5_SEGMENTS

        @pl.when(first_of_tile)
        def _():
            z = jnp.concatenate([jnp.ones((1, half), F32), jnp.zeros((1, half), F32)], axis=1)
            for i in range(seg_len + 1):
                pow1_ref[i:i + 1, :] = jnp.concatenate([z[:, :half], z[:, :half]], axis=1)
                pow2_ref[i:i + 1, :] = jnp.concatenate([-z[:, half:], z[:, half:]], axis=1)
                z = cmul(a1, a2, z)

        def local_step(i, s):
            rows = pl.ds(i, S5_SEGMENTS, stride=seg_len)
            store_rows(sprev_ref, rows, s)
            return cmul(a1, a2, s) + load_rows(xend_ref, rows)

        local_end = lax.fori_loop(0, seg_len, local_step, jnp.zeros((S5_SEGMENTS, 2 * half), F32),
                                  unroll=S5_SCAN_UNROLL)
        seg1 = pow1_ref[seg_len:seg_len + 1, :]
        seg2 = pow2_ref[seg_len:seg_len + 1, :]
        start = s_ref[...]
        for seg in range(S5_SEGMENTS):
            rows = slice(seg * seg_len, (seg + 1) * seg_len)
            store_rows(sprev_ref, rows, pow1_ref[:seg_len, :] * start + pow2_ref[:seg_len, :] * swap(start),
                       add=True)
            start = cmul(seg1, seg2, start) + local_end[seg:seg + 1, :]
        s_fin = start
    s_ref[...] = s_fin

    @pl.when(pl.program_id(2) == pl.num_programs(2) - 1)
    def _():
        reo_ref[...] = s_fin[:, :half]
        imo_ref[...] = s_fin[:, half:]

    sprev = [jnp.concatenate([sprev_ref[part * (nslab // 2) + hs * hslabs + q]
                              for part in range(2) for q in range(hslabs)], axis=1).astype(BF16)
             for hs in range(2)]
    d = d_ref[...]
    pair = 2 * LANES
    for cp in range(S5_T // 4):
        depth = (4 * cp + 4) * S5_HALF_LANES
        cols = slice(cp * pair, (cp + 1) * pair)
        y2 = [_dot(uc[hs][:, :depth], wint_ref[hs, :depth, cols]) + _dot_nt(sprev[hs], et_ref[hs, cols, :])
              for hs in range(2)]
        for i in range(2):
            m = 2 * cp + i
            tokens = halves(y2[0][:, i * LANES:(i + 1) * LANES], y2[1][:, i * LANES:(i + 1) * LANES])
            for t, y in zip((2 * m, 2 * m + 1), tokens):
                yt = _gelu(y + d * u[t])
                off = 0
                for rows in slab(t):
                    y_ref[rows, :] = yt[off:off + rows.size, :]
                    off += rows.size


def _s5(h, re0, im0, klag, f_tab, et_tab, a_t, d_skip, nseq, seq_rows, blocks_per_seq):
    tokens = h.shape[0]
    nb = re0.shape[0]
    ntile = D_MODEL // LANES
    half = S5_TILE_STATES
    rows = nseq * seq_rows
    nrow = rows // S5_T
    npow = -(-(nrow // S5_SEGMENTS + 1) // 8) * 8
    hspec = pl.BlockSpec((rows, LANES), lambda k, b, j: (b * blocks_per_seq + j, k))
    sspec = pl.BlockSpec((None, nseq, half), lambda k, b, j: (b, 0, k))
    per_tile = lambda shape: pl.BlockSpec((None,) + shape, lambda k, b, j: (k,) + (0,) * len(shape))
    kern = functools.partial(_s5_kernel, nseq=nseq, seq_rows=seq_rows)
    return pl.pallas_call(
        kern,
        grid=(ntile, nb, blocks_per_seq),
        in_specs=[hspec, sspec, sspec,
                  per_tile((S5_T, LANES, LANES)), per_tile(S5_HALF_TABLE), per_tile(S5_HALF_TABLE),
                  per_tile((2, 2 * half)),
                  pl.BlockSpec((1, LANES), lambda k, b, j: (0, k))],
        out_specs=[hspec, sspec, sspec],
        out_shape=[jax.ShapeDtypeStruct((tokens, D_MODEL), F32),
                   jax.ShapeDtypeStruct(re0.shape, F32), jax.ShapeDtypeStruct(im0.shape, F32)],
        scratch_shapes=[pltpu.VMEM((2, S5_T * S5_HALF_LANES, S5_T * S5_HALF_LANES), BF16),
                        pltpu.VMEM((2 * half // LANES, nrow, LANES), F32),
                        pltpu.VMEM((2 * half // LANES, nrow, LANES), F32),
                        pltpu.VMEM((nseq, 2 * half), F32),
                        pltpu.VMEM((npow, 2 * half), F32),
                        pltpu.VMEM((npow, 2 * half), F32)],
        compiler_params=_params("arbitrary", "arbitrary", "arbitrary"),
        name="s5",
    )(h, re0, im0, klag, f_tab, et_tab, a_t, d_skip)


def kernel(x_prompt, x_sample, cache_sb_k, cache_sb_v, state_ssm_re, state_ssm_im, norm_mix, norm_ffn,
           norm_final, ab_w_in, sgu_w, sgu_b, ab_w_out, ssm_lam_re, ssm_lam_im, ssm_log_step, ssm_b_re,
           ssm_b_im, ssm_c_re, ssm_c_im, ssm_d, ssm_w_glu, ffn_w_gate, ffn_w_up, ffn_w_down):
    bsz, seq, _ = x_prompt.shape
    dbsz, dseq, _ = x_sample.shape
    past = cache_sb_k.shape[2]
    heads = SB_WIDTH // HEAD_DIM
    row = lambda v: v.reshape(1, -1)

    xp = x_prompt.reshape(bsz * seq, D_MODEL)
    xs = x_sample.reshape(dbsz * dseq, D_MODEL)

    w_in = ab_w_in[0].astype(BF16)
    w_out = ab_w_out[0].astype(BF16)
    bs_rows = jnp.repeat(sgu_b[0].T, SGU_WIDTH // SGU_GROUPS, axis=1)
    g_mix0 = row(norm_mix[0])

    qp, kp, vp, kpb, vpb, up, gp = _proj(xp, g_mix0, w_in)
    qs, ks, vs, ksb, vsb, us, gs = _proj(xs, g_mix0, w_in)
    att_p = _sb_prompt(qp, kpb, vpb, bsz, seq)
    to_hdp = lambda c: jnp.transpose(c[0], (0, 2, 3, 1))
    att_s = _sb_sample(qs, ksb, vsb, to_hdp(cache_sb_k), to_hdp(cache_sb_v), dbsz, dseq)
    wg_all, wu_all, wd_all = _to_bf16(ffn_w_gate), _to_bf16(ffn_w_up), _to_bf16(ffn_w_down)
    ffn_w = [(wg_all[l], wu_all[l], wd_all[l]) for l in range(2)]
    g_mix1 = row(norm_mix[1])
    xp, hp = _mix_ffn(xp, att_p, up, gp, sgu_w[0], bs_rows, w_out, row(norm_ffn[0]), ffn_w[0], g_mix1)
    xs, hs = _mix_ffn(xs, att_s, us, gs, sgu_w[0][:, :dseq, :dseq], bs_rows[:dseq], w_out,
                      row(norm_ffn[0]), ffn_w[0], g_mix1)

    f_tab, et_tab, klag, a_t = _s5_tables(ssm_lam_re[0], ssm_lam_im[0], ssm_log_step[0], ssm_b_re[0],
                                          ssm_b_im[0], ssm_c_re[0], ssm_c_im[0])
    d_skip = row(ssm_d[0])
    nstates = state_ssm_re.shape[2] * state_ssm_re.shape[3]
    zeros = jnp.zeros((bsz, 1, nstates), F32)
    prompt_block = 8192
    yp, rp, ip = _s5(hp, zeros, zeros, klag, f_tab, et_tab, a_t, d_skip, 1, prompt_block, seq // prompt_block)
    ys, rs, is_ = _s5(hs, state_ssm_re[0].reshape(1, dbsz, nstates), state_ssm_im[0].reshape(1, dbsz, nstates),
                      klag, f_tab, et_tab, a_t, d_skip, dbsz, dseq, 1)
    w_glu = ssm_w_glu[0].astype(BF16)
    y_prompt = _glu_ffn(xp, yp, w_glu, row(norm_ffn[1]), ffn_w[1], row(norm_final))
    y_sample = _glu_ffn(xs, ys, w_glu, row(norm_ffn[1]), ffn_w[1], row(norm_final))

    state_shape = state_ssm_re.shape[2:]
    return (y_prompt.reshape(bsz, seq, D_MODEL), y_sample.reshape(dbsz, dseq, D_MODEL),
            kp.reshape(1, bsz, seq, heads, HEAD_DIM), vp.reshape(1, bsz, seq, heads, HEAD_DIM),
            ks.reshape(1, dbsz, dseq, heads, HEAD_DIM), vs.reshape(1, dbsz, dseq, heads, HEAD_DIM),
            gs.reshape(1, dbsz, dseq, SGU_WIDTH),
            rp.reshape((1, bsz) + state_shape), ip.reshape((1, bsz) + state_shape),
            rs.reshape((1, dbsz) + state_shape), is_.reshape((1, dbsz) + state_shape))
```

```python
import functools
import math

import jax
import jax.numpy as jnp
from jax import lax
from jax.experimental import pallas as pl
from jax.experimental.pallas import tpu as pltpu

F32 = jnp.float32
BF16 = jnp.bfloat16

LANES = 128
VMEM_LIMIT = 56 * 1024 * 1024

D_MODEL = 1024
HEAD_DIM = 64
SB_WIDTH = 512
SGU_WIDTH = 512
SGU_GROUPS = 8
SGU_CHUNK = 128
SSM_GROUP = 16
SSM_STATE = 64
FFN_HIDDEN = 2816
RMS_EPS = 1e-6

TOKEN_TILE = 512
SB_BLOCK = 128
SB_HEADS = SB_WIDTH // HEAD_DIM
SB_PAIRS = SB_WIDTH // LANES
SB_CUTOFF = -104.0
SB_PROMPT_FUSED_BLOCKS = 3
S5_T = 16
S5_SCAN_UNROLL = 8
S5_SEGMENTS = 8
S5_GROUPS_PER_TILE = LANES // SSM_GROUP
S5_TILE_STATES = S5_GROUPS_PER_TILE * SSM_STATE
S5_HALF_LANES = LANES // 2
S5_HALF_STATES = S5_TILE_STATES // 2
S5_HALF_TABLE = (2, S5_T * S5_HALF_LANES, 2 * S5_HALF_STATES)


def _rmsnorm(x, g):
    return x * lax.rsqrt(jnp.mean(x * x, axis=-1, keepdims=True) + RMS_EPS) * g


def _gelu(x):
    return 0.5 * x * (1.0 + jnp.tanh(math.sqrt(2.0 / math.pi) * (x + 0.044715 * (x * x * x))))


def _sigmoid(x):
    return 1.0 / (1.0 + jnp.exp(-x))


def _dot(a, b):
    return jnp.dot(a, b, preferred_element_type=F32)


def _dot_nt(a, b):
    return lax.dot_general(a, b, (((1,), (1,)), ((), ())), preferred_element_type=F32)


def _params(*sem):
    return pltpu.CompilerParams(dimension_semantics=sem, vmem_limit_bytes=VMEM_LIMIT)


def _resident(shape):
    nd = len(shape)
    return pl.BlockSpec(shape, lambda *_: (0,) * nd, pipeline_mode=pl.Buffered(1))


def _proj_kernel(x_ref, g_ref, w_ref, q_ref, k_ref, v_ref, kb_ref, vb_ref, u_ref, gv_ref):
    h = _rmsnorm(x_ref[...], g_ref[...]).astype(BF16)
    col = lambda i: _dot(h, w_ref[:, i * SB_WIDTH:(i + 1) * SB_WIDTH])
    for f32_ref, bf16_ref, z in ((k_ref, kb_ref, col(1)), (v_ref, vb_ref, col(2))):
        bf16_ref[...] = z.astype(BF16)
        f32_ref[...] = z.reshape(f32_ref.shape)
    q_ref[...] = (col(0) * (HEAD_DIM ** -0.5)).astype(BF16)
    u_ref[...] = _gelu(col(3))
    gv_ref[...] = _gelu(col(4))


def _proj(x, g, w_bf16):
    m = x.shape[0]
    tile = lambda w: pl.BlockSpec((TOKEN_TILE, w), lambda i: (i, 0))
    heads = pl.BlockSpec((TOKEN_TILE, SB_HEADS, HEAD_DIM), lambda i: (i, 0, 0))
    flat = lambda dt: jax.ShapeDtypeStruct((m, SB_WIDTH), dt)
    by_head = jax.ShapeDtypeStruct((m, SB_HEADS, HEAD_DIM), F32)
    return pl.pallas_call(
        _proj_kernel,
        grid=(m // TOKEN_TILE,),
        in_specs=[tile(D_MODEL), _resident((1, D_MODEL)), _resident(w_bf16.shape)],
        out_specs=[tile(SB_WIDTH), heads, heads] + [tile(SB_WIDTH)] * 4,
        out_shape=[flat(BF16), by_head, by_head, flat(BF16), flat(BF16), flat(F32), flat(F32)],
        compiler_params=_params("parallel"),
        name="proj",
    )(x, g, w_bf16)


def _suffix_and_total():
    j = lax.broadcasted_iota(jnp.int32, (SB_BLOCK, 2 * SB_BLOCK), 0)
    s = lax.broadcasted_iota(jnp.int32, (SB_BLOCK, 2 * SB_BLOCK), 1)
    return jnp.where(jnp.logical_or(j > s, s >= SB_BLOCK), 1.0, 0.0).astype(BF16)


def _sb_weights(z, c, sums_mat, masks):
    rows = z.shape[0]
    nb = len(masks)
    log_beta = jnp.minimum(z, 0.0) - jnp.log(1.0 + jnp.exp(-jnp.abs(z)))
    log_stay = log_beta - z
    parts = []
    for j, mask in enumerate(masks):
        stay = log_stay[:, j * SB_BLOCK:(j + 1) * SB_BLOCK]
        if mask is not None:
            stay = jnp.where(mask, stay, 0.0)
        parts.append(stay.astype(BF16))
    sums = _dot(parts[0] if nb == 1 else jnp.concatenate(parts, axis=0), sums_mat)
    ws = [None] * nb
    for j in reversed(range(nb)):
        s = sums[j * rows:(j + 1) * rows]
        w = jnp.exp(log_beta[:, j * SB_BLOCK:(j + 1) * SB_BLOCK] + s[:, :SB_BLOCK] + c)
        if masks[j] is not None:
            w = jnp.where(masks[j], w, 0.0)
        ws[j] = w.astype(BF16)
        c = c + s[:, SB_BLOCK:]
    return (ws[0] if nb == 1 else jnp.concatenate(ws, axis=1)), c


def _sb_alive(c):
    return jnp.max(c) > SB_CUTOFF


def _sb_pair_block(q_ref, o_ref, sums_mat):
    nq = q_ref.shape[0]
    lane = lax.broadcasted_iota(jnp.int32, (nq, LANES), 1)
    q_pairs = []
    for p in range(SB_PAIRS):
        q = q_ref[:, p * LANES:(p + 1) * LANES]
        zero = jnp.zeros_like(q)
        q_pairs.append(jnp.concatenate([jnp.where(lane < HEAD_DIM, q, zero),
                                        jnp.where(lane < HEAD_DIM, zero, q)], axis=0))

    def block(kblk, vblk, c, masks, first, transposed=False):
        qk = _dot if transposed else _dot_nt
        pv_dot = _dot_nt if transposed else _dot
        z = jnp.concatenate([qk(q_pairs[p], kblk[p]) for p in range(SB_PAIRS)], axis=0)
        w, c = _sb_weights(z, c, sums_mat, masks)
        for p in range(SB_PAIRS):
            cols = slice(p * LANES, (p + 1) * LANES)
            pv = pv_dot(w[2 * p * nq:(2 * p + 2) * nq], vblk[p])
            merged = jnp.where(lane < HEAD_DIM, pv[:nq], pv[nq:])
            o_ref[:, cols] = merged if first else o_ref[:, cols] + merged
        return c

    return block


def _pair_cols(x):
    return [x[:, p * LANES:(p + 1) * LANES] for p in range(SB_PAIRS)]


def _sb_causal(nq):
    row = lax.broadcasted_iota(jnp.int32, (SB_HEADS * nq, SB_BLOCK), 0)
    col = lax.broadcasted_iota(jnp.int32, (SB_HEADS * nq, SB_BLOCK), 1)
    return col < (row & (nq - 1))


def _sb_prompt_kernel(q_ref, k_ref, v_ref, o_ref):
    qb = pl.program_id(1)
    block = _sb_pair_block(q_ref, o_ref, _suffix_and_total())
    causal = _sb_causal(SB_BLOCK)
    c_init = jnp.zeros((SB_HEADS * SB_BLOCK, SB_BLOCK), F32)

    def newest(nb):
        rows = pl.ds(pl.multiple_of((qb - (nb - 1)) * SB_BLOCK, SB_BLOCK), nb * SB_BLOCK)
        return block(_pair_cols(k_ref[rows, :]), _pair_cols(v_ref[rows, :]), c_init,
                     [None] * (nb - 1) + [causal], True)

    fused = SB_PROMPT_FUSED_BLOCKS
    c = lax.cond(qb >= fused - 1, lambda: newest(fused), lambda: newest(1))

    def cond(state):
        kb, c = state
        return jnp.logical_and(kb >= 0, _sb_alive(c))

    def body(state):
        kb, c = state
        rows = pl.ds(pl.multiple_of(kb * SB_BLOCK, SB_BLOCK), SB_BLOCK)
        return kb - 1, block(_pair_cols(k_ref[rows, :]), _pair_cols(v_ref[rows, :]), c, [None], False)

    lax.while_loop(cond, body, (jnp.where(qb >= fused - 1, qb - fused, qb - 1), c))


def _sb_prompt(q, k, v, bsz, seq):
    nq = seq // SB_BLOCK
    qspec = pl.BlockSpec((SB_BLOCK, SB_WIDTH), lambda b, i: (b * nq + i, 0))
    kvspec = pl.BlockSpec((seq, SB_WIDTH), lambda b, i: (b, 0), pipeline_mode=pl.Buffered(1))
    return pl.pallas_call(
        _sb_prompt_kernel,
        grid=(bsz, nq),
        in_specs=[qspec, kvspec, kvspec],
        out_specs=qspec,
        out_shape=jax.ShapeDtypeStruct((bsz * seq, SB_WIDTH), F32),
        compiler_params=_params("parallel", "arbitrary"),
        name="sb_prompt",
    )(q, k, v)


def _sb_sample_kernel(q_ref, k_ref, v_ref, ck_hbm, cv_hbm, o_ref, kbuf, vbuf, sem):
    b = pl.program_id(0)
    nq = q_ref.shape[0]
    last = ck_hbm.shape[3] // SB_BLOCK - 1

    def fetch(kb, slot, batch=b):
        pos = pl.ds(pl.multiple_of(kb * SB_BLOCK, SB_BLOCK), SB_BLOCK)
        return (pltpu.make_async_copy(ck_hbm.at[batch, :, :, pos], kbuf.at[slot], sem.at[0, slot]),
                pltpu.make_async_copy(cv_hbm.at[batch, :, :, pos], vbuf.at[slot], sem.at[1, slot]))

    def pair_rows(buf, slot):
        return [buf[slot, 2 * p:2 * p + 2].reshape(2 * HEAD_DIM, SB_BLOCK).astype(BF16) for p in range(SB_PAIRS)]

    @pl.when(b == 0)
    def _():
        for cp in fetch(last, last % 2):
            cp.start()

    sums_mat = _suffix_and_total()
    pad = jnp.zeros((SB_BLOCK - k_ref.shape[0], SB_WIDTH), BF16)
    new_k = jnp.concatenate([k_ref[...], pad], axis=0)
    new_v = jnp.concatenate([v_ref[...], pad], axis=0)
    block = _sb_pair_block(q_ref, o_ref, sums_mat)
    c = block(_pair_cols(new_k), _pair_cols(new_v), jnp.zeros((SB_HEADS * nq, SB_BLOCK), F32),
              [_sb_causal(nq)], True)

    def cond(state):
        kb, c = state
        return jnp.logical_and(kb >= 0, _sb_alive(c))

    def body(state):
        kb, c = state
        slot = kb % 2
        for cp in fetch(kb, slot):
            cp.wait()

        @pl.when(kb >= 1)
        def _():
            for cp in fetch(kb - 1, 1 - slot):
                cp.start()

        return kb - 1, block(pair_rows(kbuf, slot), pair_rows(vbuf, slot), c, [None], False, transposed=True)

    kb_end, _ = lax.while_loop(cond, body, (jnp.int32(last), c))

    @pl.when(kb_end >= 0)
    def _():
        for cp in fetch(kb_end, kb_end % 2):
            cp.wait()

    @pl.when(b + 1 < pl.num_programs(0))
    def _():
        for cp in fetch(last, last % 2, b + 1):
            cp.start()


def _sb_sample(q, k, v, cache_k, cache_v, bsz, n):
    new = pl.BlockSpec((n, SB_WIDTH), lambda b: (b, 0))
    old = pl.BlockSpec(memory_space=pl.ANY)
    slots = pltpu.VMEM((2, SB_HEADS, HEAD_DIM, SB_BLOCK), F32)
    return pl.pallas_call(
        _sb_sample_kernel,
        grid=(bsz,),
        in_specs=[new, new, new, old, old],
        out_specs=new,
        out_shape=jax.ShapeDtypeStruct((bsz * n, SB_WIDTH), F32),
        scratch_shapes=[slots, slots, pltpu.SemaphoreType.DMA((2, 2))],
        compiler_params=_params("arbitrary"),
        name="sb_sample",
    )(q, k, v, cache_k, cache_v)


FFN_CHUNK = 256


def _ffn_tail(x, g_ref, wg_ref, wu_ref, wd_ref, gn_ref, o_refs):
    h = _rmsnorm(x, g_ref[...]).astype(BF16)
    acc = x
    for c in range(FFN_HIDDEN // FFN_CHUNK):
        cols = slice(c * FFN_CHUNK, (c + 1) * FFN_CHUNK)
        gate = _dot(h, wg_ref[:, cols])
        up = _dot(h, wu_ref[:, cols])
        act = (gate * _sigmoid(gate) * up).astype(BF16)
        acc = acc + _dot(act, wd_ref[cols, :])
    if len(o_refs) == 2:
        o_refs[0][...] = acc
    o_refs[-1][...] = _rmsnorm(acc, gn_ref[...])


def _mix_ffn_kernel(x_ref, att_ref, u_ref, gv_ref, ws_ref, bs_ref, wo_ref,
                    g_ref, wg_ref, wu_ref, wd_ref, gn_ref, xo_ref, ho_ref, sg_ref):
    chunk = ws_ref.shape[1]
    t = lax.broadcasted_iota(jnp.int32, (chunk, chunk), 0)
    s = lax.broadcasted_iota(jnp.int32, (chunk, chunk), 1)
    tri = s <= t
    ws = [jnp.where(tri, ws_ref[g], 0.0).astype(BF16) for g in range(SGU_GROUPS)]
    lane = lax.broadcasted_iota(jnp.int32, (chunk, LANES), 1)
    group_dim = SGU_WIDTH // SGU_GROUPS
    for c in range(TOKEN_TILE // chunk):
        rows = slice(c * chunk, (c + 1) * chunk)
        for p in range(SGU_WIDTH // LANES):
            cols = slice(p * LANES, (p + 1) * LANES)
            gv = gv_ref[rows, cols].astype(BF16)
            mixed = jnp.where(lane < group_dim, _dot(ws[2 * p], gv), _dot(ws[2 * p + 1], gv)) + bs_ref[:, cols]
            sg_ref[rows, cols] = (u_ref[rows, cols] * mixed).astype(BF16)
    xo_ref[...] = (x_ref[...] + _dot(att_ref[...].astype(BF16), wo_ref[:SB_WIDTH, :])
                   + _dot(sg_ref[...], wo_ref[SB_WIDTH:, :]))
    _ffn_tail(xo_ref[...], g_ref, wg_ref, wu_ref, wd_ref, gn_ref, (xo_ref, ho_ref))


def _glu_ffn_kernel(x_ref, y_ref, w_ref, g_ref, wg_ref, wu_ref, wd_ref, gn_ref, ho_ref):
    y = y_ref[...].astype(BF16)
    for c in range(D_MODEL // FFN_CHUNK):
        cols = slice(c * FFN_CHUNK, (c + 1) * FFN_CHUNK)
        ga = _dot(y, w_ref[:, cols])
        gb = _dot(y, w_ref[:, D_MODEL + c * FFN_CHUNK:D_MODEL + (c + 1) * FFN_CHUNK])
        ho_ref[:, cols] = x_ref[:, cols] + ga * _sigmoid(gb)
    _ffn_tail(ho_ref[...], g_ref, wg_ref, wu_ref, wd_ref, gn_ref, (ho_ref,))


def _ffn_specs(g, ffn_w, g_next):
    wg, wu, wd = ffn_w
    return ([_resident((1, D_MODEL)), _resident(wg.shape), _resident(wu.shape), _resident(wd.shape),
             _resident((1, D_MODEL))], (g, wg, wu, wd, g_next))


def _mix_ffn(x, att, u, gv, ws, bs_rows, wo_bf16, g, ffn_w, g_next):
    m = x.shape[0]
    tile = lambda w: pl.BlockSpec((TOKEN_TILE, w), lambda i: (i, 0))
    ffn_specs, ffn_args = _ffn_specs(g, ffn_w, g_next)
    return pl.pallas_call(
        _mix_ffn_kernel,
        grid=(m // TOKEN_TILE,),
        in_specs=[tile(D_MODEL), tile(SB_WIDTH), tile(SGU_WIDTH), tile(SGU_WIDTH),
                  _resident(ws.shape), _resident(bs_rows.shape), _resident(wo_bf16.shape)] + ffn_specs,
        out_specs=[tile(D_MODEL)] * 2,
        out_shape=[jax.ShapeDtypeStruct((m, D_MODEL), F32)] * 2,
        scratch_shapes=[pltpu.VMEM((TOKEN_TILE, SGU_WIDTH), BF16)],
        compiler_params=_params("parallel"),
        name="mix_ffn",
    )(x, att, u, gv, ws, bs_rows, wo_bf16, *ffn_args)


def _glu_ffn(x, y, w_glu_bf16, g, ffn_w, g_next):
    m = x.shape[0]
    tile = pl.BlockSpec((TOKEN_TILE, D_MODEL), lambda i: (i, 0))
    ffn_specs, ffn_args = _ffn_specs(g, ffn_w, g_next)
    return pl.pallas_call(
        _glu_ffn_kernel,
        grid=(m // TOKEN_TILE,),
        in_specs=[tile, tile, _resident(w_glu_bf16.shape)] + ffn_specs,
        out_specs=tile,
        out_shape=jax.ShapeDtypeStruct((m, D_MODEL), F32),
        compiler_params=_params("parallel"),
        name="glu_ffn",
    )(x, y, w_glu_bf16, *ffn_args)


def _s5_tables_kernel(lr_ref, li_ref, ls_ref, bre_ref, bim_ref, cre_ref, cim_ref,
                      f_ref, et_ref, klag_ref, at_ref, f32_ref):
    half = S5_TILE_STATES
    lr, li = lr_ref[...], li_ref[...]
    step = jnp.exp(ls_ref[...])
    mag, ang = jnp.exp(lr * step), li * step
    ar, ai = mag * jnp.cos(ang), mag * jnp.sin(ang)
    den = lr * lr + li * li
    nr = ar - 1.0
    co_re = (nr * lr + ai * li) / den
    co_im = (ai * lr - nr * li) / den

    row_group = lax.broadcasted_iota(jnp.int32, (LANES, half), 0) // SSM_GROUP
    col_group = lax.broadcasted_iota(jnp.int32, (LANES, half), 1) // SSM_STATE

    def block_diag(ref):
        return jnp.where(row_group == col_group, jnp.concatenate([ref[...]] * S5_GROUPS_PER_TILE, axis=1), 0.0)

    b_re, b_im, c_re, c_im = (block_diag(r) for r in (bre_ref, bim_ref, cre_ref, cim_ref))
    g_re = co_re * b_re - co_im * b_im
    g_im = co_re * b_im + co_im * b_re
    p_re, p_im = ar, ai
    def put_halves(ref, block, re, im):
        for hs in range(2):
            lanes = slice(hs * S5_HALF_LANES, (hs + 1) * S5_HALF_LANES)
            states = slice(hs * S5_HALF_STATES, (hs + 1) * S5_HALF_STATES)
            rows = slice(block * S5_HALF_LANES, (block + 1) * S5_HALF_LANES)
            ref[hs, rows, :S5_HALF_STATES] = re[lanes, states].astype(BF16)
            ref[hs, rows, S5_HALF_STATES:] = im[lanes, states].astype(BF16)

    for tau in range(S5_T):
        rows = slice((S5_T - 1 - tau) * LANES, (S5_T - tau) * LANES)
        f32_ref[rows, :half] = g_re
        f32_ref[rows, half:] = g_im
        put_halves(f_ref, S5_T - 1 - tau, g_re, g_im)
        put_halves(et_ref, tau, c_re * p_re - c_im * p_im, -c_re * p_im - c_im * p_re)
        if tau < S5_T - 1:
            g_re, g_im = g_re * ar - g_im * ai, g_re * ai + g_im * ar
            p_re, p_im = p_re * ar - p_im * ai, p_re * ai + p_im * ar
    at_ref[0:1, :] = jnp.concatenate([p_re, p_re], axis=1)
    at_ref[1:2, :] = jnp.concatenate([-p_im, p_im], axis=1)

    f_all = f32_ref[...]
    f_hi = f_all.astype(BF16)
    f_lo = (f_all - f_hi.astype(F32)).astype(BF16)
    cc = jnp.concatenate([c_re, -c_im], axis=1)
    c_hi = cc.astype(BF16)
    c_lo = (cc - c_hi.astype(F32)).astype(BF16)
    k_all = _dot_nt(f_hi, c_hi) + _dot_nt(f_hi, c_lo) + _dot_nt(f_lo, c_hi)
    for tau in range(S5_T):
        rows = slice((S5_T - 1 - tau) * LANES, (S5_T - tau) * LANES)
        klag_ref[tau] = k_all[rows, :]


def _s5_tables(lam_re, lam_im, log_step, b_re, b_im, c_re, c_im):
    groups, nstate = lam_re.shape
    ntile = groups // S5_GROUPS_PER_TILE
    half = S5_TILE_STATES
    vec = lambda a: a.reshape(ntile, 1, half)
    mat = lambda a: a.reshape(ntile, LANES, nstate)
    args = (vec(lam_re), vec(lam_im), vec(jnp.repeat(log_step, nstate)),
            mat(jnp.swapaxes(b_re, 1, 2)), mat(jnp.swapaxes(b_im, 1, 2)), mat(c_re), mat(c_im))
    per_tile = lambda shape: pl.BlockSpec((None,) + shape, lambda k: (k,) + (0,) * len(shape))
    return pl.pallas_call(
        _s5_tables_kernel,
        grid=(ntile,),
        in_specs=[per_tile((1, half))] * 3 + [per_tile((LANES, nstate))] * 4,
        out_specs=[per_tile(S5_HALF_TABLE), per_tile(S5_HALF_TABLE),
                   per_tile((S5_T, LANES, LANES)), per_tile((2, 2 * half))],
        out_shape=[jax.ShapeDtypeStruct((ntile,) + S5_HALF_TABLE, BF16),
                   jax.ShapeDtypeStruct((ntile,) + S5_HALF_TABLE, BF16),
                   jax.ShapeDtypeStruct((ntile, S5_T, LANES, LANES), F32),
                   jax.ShapeDtypeStruct((ntile, 2, 2 * half), F32)],
        scratch_shapes=[pltpu.VMEM((S5_T * LANES, 2 * half), F32)],
        compiler_params=_params("parallel"),
        name="s5_tables",
    )(*args)


def _s5_kernel(h_ref, re0_ref, im0_ref, klag_ref, f_ref, et_ref, at_ref, d_ref,
               y_ref, reo_ref, imo_ref, wint_ref, xend_ref, sprev_ref, s_ref, pow1_ref, pow2_ref,
               *, nseq, seq_rows):
    first_of_tile = jnp.logical_and(pl.program_id(1) == 0, pl.program_id(2) == 0)
    nchunk = seq_rows // S5_T
    half = S5_TILE_STATES

    def halves(a, b):
        lane = lax.broadcasted_iota(jnp.int32, a.shape, 1)
        first = lane < S5_HALF_LANES
        return (jnp.where(first, a, pltpu.roll(b, S5_HALF_LANES, 1)),
                jnp.where(first, pltpu.roll(a, S5_HALF_LANES, 1), b))

    @pl.when(first_of_tile)
    def _():
        wint_ref[...] = jnp.zeros_like(wint_ref)
        zero = jnp.zeros((LANES, LANES), F32)
        for j in range(S5_T):
            for m in range(j // 2, S5_T // 2):
                lag = 2 * m - j
                low, high = halves(klag_ref[lag] if lag >= 0 else zero, klag_ref[lag + 1])
                rows = slice(j * S5_HALF_LANES, (j + 1) * S5_HALF_LANES)
                wint_ref[0, rows, m * LANES:(m + 1) * LANES] = low[:S5_HALF_LANES].astype(BF16)
                wint_ref[1, rows, m * LANES:(m + 1) * LANES] = high[S5_HALF_LANES:].astype(BF16)

    @pl.when(pl.program_id(2) == 0)
    def _():
        s_ref[:, :half] = re0_ref[...]
        s_ref[:, half:] = im0_ref[...]

    def slab(t):
        if nseq == 1:
            return [pl.ds(t, nchunk, stride=S5_T)]
        return [pl.ds(n * S5_T + t, nseq, stride=seq_rows) for n in range(nchunk)]

    u = []
    for t in range(S5_T):
        parts = [h_ref[rows, :] for rows in slab(t)]
        u.append(parts[0] if len(parts) == 1 else jnp.concatenate(parts, axis=0))
    split = [halves(u[2 * m], u[2 * m + 1]) for m in range(S5_T // 2)]
    uc = [jnp.concatenate([pair[hs].astype(BF16) for pair in split], axis=1) for hs in range(2)]

    nslab = 2 * half // LANES

    def load_rows(ref, rows):
        return jnp.concatenate([ref[q, rows, :] for q in range(nslab)], axis=1)

    def store_rows(ref, rows, val, add=False):
        for q in range(nslab):
            piece = val[:, q * LANES:(q + 1) * LANES]
            ref[q, rows, :] = ref[q, rows, :] + piece if add else piece

    hstates = S5_HALF_STATES
    hslabs = hstates // LANES
    for hs in range(2):
        x = _dot(uc[hs], f_ref[hs])
        for part in range(2):
            for q in range(hslabs):
                col = part * hstates + q * LANES
                xend_ref[part * (nslab // 2) + hs * hslabs + q] = x[:, col:col + LANES]

    a1 = at_ref[0:1, :]
    a2 = at_ref[1:2, :]
    swap = lambda s: jnp.concatenate([s[:, half:], s[:, :half]], axis=1)
    cmul = lambda p1, p2, s: p1 * s + p2 * swap(s)

    if nseq > 1:
        def step(n, s):
            rows = pl.ds(n * nseq, nseq)
            store_rows(sprev_ref, rows, s)
            return cmul(a1, a2, s) + load_rows(xend_ref, rows)

        s_fin = lax.fori_loop(0, nchunk, step, s_ref[...], unroll=min(S5_SCAN_UNROLL, nchunk))
    else:
        seg_len = nchunk // S5_SEGMENTS

        @pl.when(first_of_tile)
        def _():
            z = jnp.concatenate([jnp.ones((1, half), F32), jnp.zeros((1, half), F32)], axis=1)
            for i in range(seg_len + 1):
                pow1_ref[i:i + 1, :] = jnp.concatenate([z[:, :half], z[:, :half]], axis=1)
                pow2_ref[i:i + 1, :] = jnp.concatenate([-z[:, half:], z[:, half:]], axis=1)
                z = cmul(a1, a2, z)

        def local_step(i, s):
            rows = pl.ds(i, S5_SEGMENTS, stride=seg_len)
            store_rows(sprev_ref, rows, s)
            return cmul(a1, a2, s) + load_rows(xend_ref, rows)

        local_end = lax.fori_loop(0, seg_len, local_step, jnp.zeros((S5_SEGMENTS, 2 * half), F32),
                                  unroll=S5_SCAN_UNROLL)
        seg1 = pow1_ref[seg_len:seg_len + 1, :]
        seg2 = pow2_ref[seg_len:seg_len + 1, :]
        start = s_ref[...]
        for seg in range(S5_SEGMENTS):
            rows = slice(seg * seg_len, (seg + 1) * seg_len)
            store_rows(sprev_ref, rows, pow1_ref[:seg_len, :] * start + pow2_ref[:seg_len, :] * swap(start),
                       add=True)
            start = cmul(seg1, seg2, start) + local_end[seg:seg + 1, :]
        s_fin = start
    s_ref[...] = s_fin

    @pl.when(pl.program_id(2) == pl.num_programs(2) - 1)
    def _():
        reo_ref[...] = s_fin[:, :half]
        imo_ref[...] = s_fin[:, half:]

    sprev = [jnp.concatenate([sprev_ref[part * (nslab // 2) + hs * hslabs + q]
                              for part in range(2) for q in range(hslabs)], axis=1).astype(BF16)
             for hs in range(2)]
    d = d_ref[...]
    pair = 2 * LANES
    for cp in range(S5_T // 4):
        depth = (4 * cp + 4) * S5_HALF_LANES
        cols = slice(cp * pair, (cp + 1) * pair)
        y2 = [_dot(uc[hs][:, :depth], wint_ref[hs, :depth, cols]) + _dot_nt(sprev[hs], et_ref[hs, cols, :])
              for hs in range(2)]
        for i in range(2):
            m = 2 * cp + i
            tokens = halves(y2[0][:, i * LANES:(i + 1) * LANES], y2[1][:, i * LANES:(i + 1) * LANES])
            for t, y in zip((2 * m, 2 * m + 1), tokens):
                yt = _gelu(y + d * u[t])
                off = 0
                for rows in slab(t):
                    y_ref[rows, :] = yt[off:off + rows.size, :]
                    off += rows.size


def _s5(h, re0, im0, klag, f_tab, et_tab, a_t, d_skip, nseq, seq_rows, blocks_per_seq):
    tokens = h.shape[0]
    nb = re0.shape[0]
    ntile = D_MODEL // LANES
    half = S5_TILE_STATES
    rows = nseq * seq_rows
    nrow = rows // S5_T
    npow = -(-(nrow // S5_SEGMENTS + 1) // 8) * 8
    hspec = pl.BlockSpec((rows, LANES), lambda k, b, j: (b * blocks_per_seq + j, k))
    sspec = pl.BlockSpec((None, nseq, half), lambda k, b, j: (b, 0, k))
    per_tile = lambda shape: pl.BlockSpec((None,) + shape, lambda k, b, j: (k,) + (0,) * len(shape))
    kern = functools.partial(_s5_kernel, nseq=nseq, seq_rows=seq_rows)
    return pl.pallas_call(
        kern,
        grid=(ntile, nb, blocks_per_seq),
        in_specs=[hspec, sspec, sspec,
                  per_tile((S5_T, LANES, LANES)), per_tile(S5_HALF_TABLE), per_tile(S5_HALF_TABLE),
                  per_tile((2, 2 * half)),
                  pl.BlockSpec((1, LANES), lambda k, b, j: (0, k))],
        out_specs=[hspec, sspec, sspec],
        out_shape=[jax.ShapeDtypeStruct((tokens, D_MODEL), F32),
                   jax.ShapeDtypeStruct(re0.shape, F32), jax.ShapeDtypeStruct(im0.shape, F32)],
        scratch_shapes=[pltpu.VMEM((2, S5_T * S5_HALF_LANES, S5_T * S5_HALF_LANES), BF16),
                        pltpu.VMEM((2 * half // LANES, nrow, LANES), F32),
                        pltpu.VMEM((2 * half // LANES, nrow, LANES), F32),
                        pltpu.VMEM((nseq, 2 * half), F32),
                        pltpu.VMEM((npow, 2 * half), F32),
                        pltpu.VMEM((npow, 2 * half), F32)],
        compiler_params=_params("arbitrary", "arbitrary", "arbitrary"),
        name="s5",
    )(h, re0, im0, klag, f_tab, et_tab, a_t, d_skip)


def kernel(x_prompt, x_sample, cache_sb_k, cache_sb_v, state_ssm_re, state_ssm_im, norm_mix, norm_ffn,
           norm_final, ab_w_in, sgu_w, sgu_b, ab_w_out, ssm_lam_re, ssm_lam_im, ssm_log_step, ssm_b_re,
           ssm_b_im, ssm_c_re, ssm_c_im, ssm_d, ssm_w_glu, ffn_w_gate, ffn_w_up, ffn_w_down):
    bsz, seq, _ = x_prompt.shape
    dbsz, dseq, _ = x_sample.shape
    past = cache_sb_k.shape[2]
    heads = SB_WIDTH // HEAD_DIM
    row = lambda v: v.reshape(1, -1)

    xp = x_prompt.reshape(bsz * seq, D_MODEL)
    xs = x_sample.reshape(dbsz * dseq, D_MODEL)

    w_in = ab_w_in[0].astype(BF16)
    w_out = ab_w_out[0].astype(BF16)
    bs_rows = jnp.repeat(sgu_b[0].T, SGU_WIDTH // SGU_GROUPS, axis=1)
    g_mix0 = row(norm_mix[0])

    qp, kp, vp, kpb, vpb, up, gp = _proj(xp, g_mix0, w_in)
    qs, ks, vs, ksb, vsb, us, gs = _proj(xs, g_mix0, w_in)
    att_p = _sb_prompt(qp, kpb, vpb, bsz, seq)
    to_hdp = lambda c: jnp.transpose(c[0], (0, 2, 3, 1))
    att_s = _sb_sample(qs, ksb, vsb, to_hdp(cache_sb_k), to_hdp(cache_sb_v), dbsz, dseq)
    ffn_w = [(ffn_w_gate[l].astype(BF16), ffn_w_up[l].astype(BF16), ffn_w_down[l].astype(BF16)) for l in range(2)]
    g_mix1 = row(norm_mix[1])
    xp, hp = _mix_ffn(xp, att_p, up, gp, sgu_w[0], bs_rows, w_out, row(norm_ffn[0]), ffn_w[0], g_mix1)
    xs, hs = _mix_ffn(xs, att_s, us, gs, sgu_w[0][:, :dseq, :dseq], bs_rows[:dseq], w_out,
                      row(norm_ffn[0]), ffn_w[0], g_mix1)

    f_tab, et_tab, klag, a_t = _s5_tables(ssm_lam_re[0], ssm_lam_im[0], ssm_log_step[0], ssm_b_re[0],
                                          ssm_b_im[0], ssm_c_re[0], ssm_c_im[0])
    d_skip = row(ssm_d[0])
    nstates = state_ssm_re.shape[2] * state_ssm_re.shape[3]
    zeros = jnp.zeros((bsz, 1, nstates), F32)
    prompt_block = 8192
    yp, rp, ip = _s5(hp, zeros, zeros, klag, f_tab, et_tab, a_t, d_skip, 1, prompt_block, seq // prompt_block)
    ys, rs, is_ = _s5(hs, state_ssm_re[0].reshape(1, dbsz, nstates), state_ssm_im[0].reshape(1, dbsz, nstates),
                      klag, f_tab, et_tab, a_t, d_skip, dbsz, dseq, 1)
    w_glu = ssm_w_glu[0].astype(BF16)
    y_prompt = _glu_ffn(xp, yp, w_glu, row(norm_ffn[1]), ffn_w[1], row(norm_final))
    y_sample = _glu_ffn(xs, ys, w_glu, row(norm_ffn[1]), ffn_w[1], row(norm_final))

    state_shape = state_ssm_re.shape[2:]
    return (y_prompt.reshape(bsz, seq, D_MODEL), y_sample.reshape(dbsz, dseq, D_MODEL),
            kp.reshape(1, bsz, seq, heads, HEAD_DIM), vp.reshape(1, bsz, seq, heads, HEAD_DIM),
            ks.reshape(1, dbsz, dseq, heads, HEAD_DIM), vs.reshape(1, dbsz, dseq, heads, HEAD_DIM),
            gs.reshape(1, dbsz, dseq, SGU_WIDTH),
            rp.reshape((1, bsz) + state_shape), ip.reshape((1, bsz) + state_shape),
            rs.reshape((1, dbsz) + state_shape), is_.reshape((1, dbsz) + state_shape))
```
